```python
import math
import jax, jax.numpy as jnp
from jax import lax
import numpy as np

D_MODEL = 2048
BATCH = 4
SEQ = 2048
DEPTH = 2

NORM_EPS = 1e-6
N_BRANCH = 4
BRANCH_WIDTH = D_MODEL // 2

NUM_BUCKETS = 32
MAX_DISTANCE = 128

A_QK_DIM = 64
A_V_DIM = 2 * A_QK_DIM
A_HEADS = BRANCH_WIDTH // A_V_DIM
A_WIDTH = A_HEADS * A_V_DIM
Q_BLOCK = 128

B_HEAD_DIM = 64
B_HEADS = BRANCH_WIDTH // B_HEAD_DIM
B_KV_HEADS = B_HEADS // 4
B_WIDTH = B_HEADS * B_HEAD_DIM
WINDOW = 128

C_HEAD_DIM = 64
C_HEADS = BRANCH_WIDTH // C_HEAD_DIM
C_WIDTH = C_HEADS * C_HEAD_DIM
C_GROUPS = 2
C_STATE = 128
C_CONV = 4
CHUNK = 128
C_CONV_CH = C_WIDTH + 2 * C_GROUPS * C_STATE

D_WIDTH = BRANCH_WIDTH
D_CONV = 31

IN_SPLITS = (
    2 * A_HEADS * A_QK_DIM, 2 * A_HEADS * A_QK_DIM, A_WIDTH, A_WIDTH,
    B_WIDTH, B_KV_HEADS * B_HEAD_DIM, B_KV_HEADS * B_HEAD_DIM, B_WIDTH,
    C_CONV_CH, C_HEADS, C_WIDTH,
    2 * D_WIDTH, D_WIDTH,
    N_BRANCH * D_MODEL,
)
N_IN = sum(IN_SPLITS)

kernel_name = "hybrid_diffattn_swa_ssd_conformer"


def _split_points():
    pts, acc = [], 0
    for n in IN_SPLITS[:-1]:
        acc += n
        pts.append(acc)
    return pts


def rms_norm(x, w):
    xf = x.astype(jnp.float32)
    y = xf * lax.rsqrt(jnp.mean(xf * xf, axis=-1, keepdims=True) + NORM_EPS)
    return (y * w.astype(jnp.float32)).astype(x.dtype)


def layer_norm(x, w, b):
    xf = x.astype(jnp.float32)
    mu = jnp.mean(xf, axis=-1, keepdims=True)
    xc = xf - mu
    var = jnp.mean(xc * xc, axis=-1, keepdims=True)
    y = xc * lax.rsqrt(var + NORM_EPS) * w.astype(jnp.float32) + b.astype(jnp.float32)
    return y.astype(x.dtype)


def causal_depthwise_conv(x, w, b):
    k = w.shape[0]
    y = lax.conv_general_dilated(
        x, w[:, None, :].astype(x.dtype), window_strides=(1,), padding=[(k - 1, 0)],
        dimension_numbers=("NWC", "WIO", "NWC"), feature_group_count=x.shape[-1])
    return y + b.astype(x.dtype)


def t5_bucket(dist):
    n = jnp.maximum(dist, 0)
    max_exact = NUM_BUCKETS // 2
    nf = jnp.maximum(n, 1).astype(jnp.float32)
    large = max_exact + (jnp.log(nf / max_exact) / math.log(MAX_DISTANCE / max_exact)
                         * (NUM_BUCKETS - max_exact)).astype(jnp.int32)
    large = jnp.minimum(large, NUM_BUCKETS - 1)
    return jnp.where(n < max_exact, n, large)


def diff_attention(q, k, v, lam, lam_init, subln_w, bias_table):
    b, s = q.shape[:2]
    nblk = s // Q_BLOCK
    q_blocks = (q * (A_QK_DIM ** -0.5)).reshape(
        b, nblk, Q_BLOCK, A_HEADS, 2, A_QK_DIM).swapaxes(0, 1)
    k_pos = jnp.arange(s)

    def block(args):
        qb, start = args
        q_pos = start + jnp.arange(Q_BLOCK)
        dist = q_pos[:, None] - k_pos[None, :]
        bias = bias_table[t5_bucket(dist)].astype(jnp.float32)
        bias = jnp.where((dist >= 0)[..., None], bias, -jnp.inf).transpose(2, 0, 1)
        logits = jnp.einsum("bqhmd,bkhmd->bhmqk", qb, k).astype(jnp.float32) + bias[:, None]
        p = jax.nn.softmax(logits, axis=-1)
        a = (p[:, :, 0] - lam * p[:, :, 1]).astype(v.dtype)
        return jnp.einsum("bhqk,bkhe->bqhe", a, v)

    starts = jnp.arange(nblk) * Q_BLOCK
    out = lax.map(block, (q_blocks, starts))
    out = out.swapaxes(0, 1).reshape(b, s, A_HEADS, A_V_DIM)
    out = rms_norm(out, subln_w) * (1.0 - lam_init)
    return out.reshape(b, s, A_WIDTH)


def sliding_window_attention(q, k, v, sinks, bias_table):
    b, s = q.shape[:2]
    nb = s // WINDOW
    g = B_HEADS // B_KV_HEADS
    qb = (q * (B_HEAD_DIM ** -0.5)).reshape(b, nb, WINDOW, B_KV_HEADS, g, B_HEAD_DIM)

    def band(t):
        tb = t.reshape(b, nb, WINDOW, B_KV_HEADS, B_HEAD_DIM)
        prev = jnp.concatenate([jnp.zeros_like(tb[:, :1]), tb[:, :-1]], axis=1)
        return jnp.concatenate([prev, tb], axis=2)

    kb, vb = band(k), band(v)
    qi = jnp.arange(WINDOW)[:, None]
    kj = jnp.arange(2 * WINDOW)[None, :]
    dist = WINDOW + qi - kj
    k_pos = jnp.arange(nb)[:, None, None] * WINDOW - WINDOW + kj
    valid = (dist >= 0) & (dist < WINDOW) & (k_pos >= 0)
    bias = bias_table[t5_bucket(dist)].astype(jnp.float32)
    bias = bias.transpose(2, 0, 1).reshape(B_KV_HEADS, g, WINDOW, 2 * WINDOW)
    logits = jnp.einsum("bnqhgd,bnkhd->bnhgqk", qb, kb).astype(jnp.float32) + bias
    logits = jnp.where(valid[:, None, None], logits, -jnp.inf)
    sink = sinks.reshape(B_KV_HEADS, g).astype(jnp.float32)[:, :, None, None]
    sink = jnp.broadcast_to(sink, logits.shape[:-1] + (1,))
    p = jax.nn.softmax(jnp.concatenate([logits, sink], axis=-1), axis=-1)[..., :-1]
    out = jnp.einsum("bnhgqk,bnkhd->bnqhgd", p.astype(vb.dtype), vb)
    return out.reshape(b, s, B_WIDTH)


def ssd_mixer(xbc, dt_raw, z, conv_w, conv_b, dt_bias, a_log, d_skip, norm_w):
    b, s, _ = xbc.shape
    xbc = jax.nn.silu(causal_depthwise_conv(xbc, conv_w, conv_b))
    xs, bm, cm = jnp.split(xbc, [C_WIDTH, C_WIDTH + C_GROUPS * C_STATE], axis=-1)
    nc = s // CHUNK
    e = C_HEADS // C_GROUPS
    x = xs.reshape(b, nc, CHUNK, C_GROUPS, e, C_HEAD_DIM)
    bm = bm.reshape(b, nc, CHUNK, C_GROUPS, C_STATE)
    cm = cm.reshape(b, nc, CHUNK, C_GROUPS, C_STATE)
    dt = jax.nn.softplus((dt_raw + dt_bias).astype(jnp.float32))
    a = -jnp.exp(a_log.astype(jnp.float32))
    dt_c = dt.reshape(b, nc, CHUNK, C_GROUPS, e)
    a_cum = jnp.cumsum((dt_c * a.reshape(C_GROUPS, e)).transpose(0, 3, 4, 1, 2), axis=-1)
    xdt = x * dt_c[..., None].astype(x.dtype)
    seg = a_cum[..., :, None] - a_cum[..., None, :]
    causal = jnp.tril(jnp.ones((CHUNK, CHUNK), dtype=bool))
    decay = jnp.exp(jnp.where(causal, seg, -jnp.inf)).astype(x.dtype)
    cb = jnp.einsum("bclgn,bcsgn->bgcls", cm, bm)
    y_diag = jnp.einsum("bgcls,bgecls,bcsgep->bclgep", cb, decay, xdt)
    decay_states = jnp.exp(a_cum[..., -1:] - a_cum).astype(x.dtype)
    states = jnp.einsum("bclgn,bgecl,bclgep->bcgepn", bm, decay_states, xdt)
    chunk_decay = jnp.exp(a_cum[..., -1]).astype(x.dtype)

    def step(h, inp):
        dec, st = inp
        return dec[..., None, None] * h + st, h

    h0 = jnp.zeros_like(states[:, 0])
    _, prev = lax.scan(step, h0, (jnp.moveaxis(chunk_decay, -1, 0), jnp.moveaxis(states, 1, 0)))
    prev = jnp.moveaxis(prev, 0, 1)
    y_off = jnp.einsum("bclgn,bcgepn,bgecl->bclgep", cm, prev, jnp.exp(a_cum).astype(x.dtype))
    y = (y_diag + y_off + x * d_skip.reshape(C_GROUPS, e)[..., None].astype(x.dtype))
    y = y.reshape(b, s, C_WIDTH)
    yg = (y * jax.nn.silu(z)).reshape(b, s, C_GROUPS, C_WIDTH // C_GROUPS)
    return rms_norm(yg, norm_w.reshape(C_GROUPS, -1)).reshape(b, s, C_WIDTH)


def conformer_conv(glu_in, conv_w, conv_b, ln_w, ln_b):
    val, gate = jnp.split(glu_in, 2, axis=-1)
    h = val * jax.nn.sigmoid(gate)
    h = causal_depthwise_conv(h, conv_w, conv_b)
    h = layer_norm(h, ln_w, ln_b)
    return jax.nn.silu(h)


def setup_inputs(seed: int = 0) -> dict:
    key = jax.random.key(seed)
    ks = jax.random.split(key, 24)
    f32 = jnp.float32
    nrm = lambda k, shape, sc: jax.random.normal(k, shape, f32) * sc
    dt = jnp.exp(jax.random.uniform(ks[8], (DEPTH, C_HEADS), f32,
                                    minval=math.log(1e-3), maxval=math.log(1e-1)))
    return {
        "x": nrm(ks[0], (BATCH, SEQ, D_MODEL), 1.0),
        "norm_w": 1.0 + nrm(ks[1], (DEPTH, D_MODEL), 0.02),
        "w_in": nrm(ks[2], (DEPTH, D_MODEL, N_IN), D_MODEL ** -0.5),
        "diff_lambda": nrm(ks[3], (DEPTH, 4, A_QK_DIM), 0.1),
        "diff_subln_w": 1.0 + nrm(ks[4], (DEPTH, A_V_DIM), 0.02),
        "swa_sinks": nrm(ks[5], (DEPTH, B_HEADS), 0.5),
        "ssd_conv_w": nrm(ks[6], (DEPTH, C_CONV, C_CONV_CH), C_CONV ** -0.5),
        "ssd_conv_b": nrm(ks[7], (DEPTH, C_CONV_CH), 0.02),
        "ssd_dt_bias": dt + jnp.log(-jnp.expm1(-dt)),
        "ssd_a_log": jnp.log(jax.random.uniform(ks[9], (DEPTH, C_HEADS), f32, minval=1.0, maxval=16.0)),
        "ssd_d": 1.0 + nrm(ks[10], (DEPTH, C_HEADS), 0.02),
        "ssd_norm_w": 1.0 + nrm(ks[11], (DEPTH, C_WIDTH), 0.02),
        "conf_conv_w": nrm(ks[12], (DEPTH, D_CONV, D_WIDTH), D_CONV ** -0.5),
        "conf_conv_b": nrm(ks[13], (DEPTH, D_WIDTH), 0.02),
        "conf_ln_w": 1.0 + nrm(ks[14], (DEPTH, D_WIDTH), 0.02),
        "conf_ln_b": nrm(ks[15], (DEPTH, D_WIDTH), 0.02),
        "w_branch": nrm(ks[16], (DEPTH, N_BRANCH, BRANCH_WIDTH, D_MODEL), BRANCH_WIDTH ** -0.5),
        "w_out": nrm(ks[17], (DEPTH, D_MODEL, D_MODEL), D_MODEL ** -0.5),
        "rel_bias": nrm(ks[18], (NUM_BUCKETS, A_HEADS + B_HEADS), 0.5),
        "final_norm_w": 1.0 + nrm(ks[19], (D_MODEL,), 0.02),
    }


def reference(x, norm_w, w_in, diff_lambda, diff_subln_w, swa_sinks, ssd_conv_w, ssd_conv_b,
              ssd_dt_bias, ssd_a_log, ssd_d, ssd_norm_w, conf_conv_w, conf_conv_b, conf_ln_w,
              conf_ln_b, w_branch, w_out, rel_bias, final_norm_w):
    b, s, _ = x.shape
    bias_a = rel_bias[:, :A_HEADS]
    bias_b = rel_bias[:, A_HEADS:]
    for l in range(DEPTH):
        h = rms_norm(x, norm_w[l])
        proj = jnp.einsum("bsd,dn->bsn", h, w_in[l])
        (aq, ak, av, ag, bq, bk, bv, bg, cxbc, cdt, cz, dglu, dg, mg) = jnp.split(
            proj, _split_points(), axis=-1)

        lam_init = 0.8 - 0.6 * math.exp(-0.3 * l)
        lq1, lk1, lq2, lk2 = (diff_lambda[l, i].astype(jnp.float32) for i in range(4))
        lam = jnp.exp(jnp.sum(lq1 * lk1)) - jnp.exp(jnp.sum(lq2 * lk2)) + lam_init
        ya = diff_attention(aq.reshape(b, s, A_HEADS, 2, A_QK_DIM),
                            ak.reshape(b, s, A_HEADS, 2, A_QK_DIM),
                            av.reshape(b, s, A_HEADS, A_V_DIM),
                            lam, lam_init, diff_subln_w[l], bias_a)
        ya = ya * jax.nn.silu(ag)

        g = B_HEADS // B_KV_HEADS
        yb = sliding_window_attention(bq.reshape(b, s, B_KV_HEADS, g, B_HEAD_DIM),
                                      bk.reshape(b, s, B_KV_HEADS, B_HEAD_DIM),
                                      bv.reshape(b, s, B_KV_HEADS, B_HEAD_DIM),
                                      swa_sinks[l], bias_b)
        yb = yb * jax.nn.silu(bg)

        yc = ssd_mixer(cxbc, cdt, cz, ssd_conv_w[l], ssd_conv_b[l], ssd_dt_bias[l],
                       ssd_a_log[l], ssd_d[l], ssd_norm_w[l])

        yd = conformer_conv(dglu, conf_conv_w[l], conf_conv_b[l], conf_ln_w[l], conf_ln_b[l])
        yd = yd * jax.nn.silu(dg)

        branches = jnp.stack([ya, yb, yc, yd], axis=2)
        up = jnp.einsum("bsir,ird->bsid", branches, w_branch[l])
        gates = jax.nn.sigmoid(mg.reshape(b, s, N_BRANCH, D_MODEL))
        merged = jnp.sum(gates * up, axis=2)
        x = x + jnp.einsum("bsd,de->bse", merged, w_out[l])
    return rms_norm(x, final_norm_w)
```

```python
import functools
import math

import jax
import jax.numpy as jnp
import numpy as np
from jax import lax
from jax.experimental import pallas as pl
from jax.experimental.pallas import tpu as pltpu

F32 = jnp.float32
BF16 = jnp.bfloat16

D_MODEL = 2048
NORM_EPS = 1e-6
BRANCH_WIDTH = 1024
NUM_BUCKETS = 32
MAX_DISTANCE = 128

A_QK = 64
A_V = 128
A_HEADS = 8
B_HD = 64
B_HEADS = 16
B_KV = 4
WINDOW = 128
C_HEADS = 16
C_HD = 64
C_GROUPS = 2
C_STATE = 128
C_CONV = 4
CHUNK = 128
C_XBC = 1536
D_CONV = 31
N_BRANCH = 4

COL_A = 0
COL_BQ = 4096
COL_BG = 5120
COL_CXBC = 6144
COL_BK = 7680
COL_BV = 7936
COL_CZ = 8192
COL_DG = 9216
COL_DGLU = 10240
N_MIX = 12288
_SRC_RANGES = (
    (0, 4096),
    (4096, 5120),
    (5632, 6656),
    (6656, 8192),
    (5120, 5376),
    (5376, 5632),
    (8208, 9232),
    (11280, 12304),
    (9232, 11280),
)
SRC_DT = (8192, 8208)
SRC_MG = 12304

NEG = -1e30
LANES = 128
VMEM_LIMIT = 56 * 1024 * 1024


def _cparams(*sem):
    return pltpu.CompilerParams(dimension_semantics=sem, vmem_limit_bytes=VMEM_LIMIT)


def _sigmoid(x):
    return 1.0 / (1.0 + jnp.exp(-x))


def _silu(x):
    return x * _sigmoid(x)


def _rmsnorm_kernel(x_ref, w_ref, o_ref):
    x = x_ref[...]
    ms = jnp.mean(x * x, axis=-1, keepdims=True)
    o_ref[...] = (x * lax.rsqrt(ms + NORM_EPS) * w_ref[...]).astype(o_ref.dtype)


def _rmsnorm(x2, w, out_dtype):
    t, d = x2.shape
    tm = min(512, t)
    return pl.pallas_call(
        _rmsnorm_kernel,
        grid=(t // tm,),
        in_specs=[pl.BlockSpec((tm, d), lambda i: (i, 0)),
                  pl.BlockSpec((1, d), lambda i: (0, 0))],
        out_specs=pl.BlockSpec((tm, d), lambda i: (i, 0)),
        out_shape=jax.ShapeDtypeStruct((t, d), out_dtype),
        compiler_params=_cparams("parallel"),
        name="rmsnorm",
    )(x2, w.reshape(1, d))


def _matmul_kernel(a_ref, b_ref, o_ref):
    o_ref[...] = jnp.dot(a_ref[...], b_ref[...],
                         preferred_element_type=F32).astype(o_ref.dtype)


def _matmul(a, b, out_dtype, tm, tn, name):
    m, k = a.shape
    n = b.shape[1]
    tm = min(tm, m)
    tn = min(tn, n)
    return pl.pallas_call(
        _matmul_kernel,
        grid=(m // tm, n // tn),
        in_specs=[pl.BlockSpec((tm, k), lambda i, j: (i, 0)),
                  pl.BlockSpec((k, tn), lambda i, j: (0, j))],
        out_specs=pl.BlockSpec((tm, tn), lambda i, j: (i, j)),
        out_shape=jax.ShapeDtypeStruct((m, n), out_dtype),
        compiler_params=_cparams("parallel", "parallel"),
        name=name,
    )(a, b)


def _t5_bucket(dist):
    n = jnp.maximum(dist, 0)
    max_exact = NUM_BUCKETS // 2
    nf = jnp.maximum(n, 1).astype(F32)
    large = max_exact + (jnp.log(nf / max_exact) / math.log(MAX_DISTANCE / max_exact)
                         * (NUM_BUCKETS - max_exact)).astype(jnp.int32)
    large = jnp.minimum(large, NUM_BUCKETS - 1)
    return jnp.where(n < max_exact, n, large)


def _online_update(s, v_blk, m_ref, l_ref, acc_ref, reps):
    m_prev = m_ref[...]
    m_next = jnp.maximum(m_prev, jnp.max(s, axis=1, keepdims=True))
    p = jnp.exp(s - jnp.concatenate([m_next] * reps, axis=1))
    alpha = jnp.exp(m_prev - m_next)
    l_ref[...] = alpha * l_ref[...] + jnp.sum(p, axis=1, keepdims=True)
    acc_ref[...] = alpha * acc_ref[...] + jnp.dot(
        p.astype(BF16), v_blk, preferred_element_type=F32)
    m_ref[...] = m_next


def _diffattn_kernel(cfar_ref, lam_ref, q_ref, k_ref, v_ref, g_ref, bias_ref, subw_ref,
                     o_ref, m1, l1, acc1, m2, l2, acc2, *, tq, lam_init):
    h = pl.program_id(1)
    i = pl.program_id(2)
    reps = tq // LANES
    scale = A_QK ** -0.5

    for m_ref, l_ref, acc_ref in ((m1, l1, acc1), (m2, l2, acc2)):
        m_ref[...] = jnp.full(m_ref.shape, NEG, F32)
        l_ref[...] = jnp.zeros(l_ref.shape, F32)
        acc_ref[...] = jnp.zeros(acc_ref.shape, F32)

    q = q_ref[...]
    q1 = q[:, :A_QK]
    q2 = q[:, A_QK:]

    def block(j, bias):
        start = pl.multiple_of(j * tq, tq)
        kj = k_ref[pl.ds(start, tq), :]
        vj = v_ref[pl.ds(start, tq), :]
        dn = (((1,), (1,)), ((), ()))
        s1 = lax.dot_general(q1, kj[:, :A_QK], dn, preferred_element_type=F32) * scale + bias
        s2 = lax.dot_general(q2, kj[:, A_QK:], dn, preferred_element_type=F32) * scale + bias
        _online_update(s1, vj, m1, l1, acc1, reps)
        _online_update(s2, vj, m2, l2, acc2, reps)

    cfar = cfar_ref[h]

    def far_body(j, carry):
        block(j, cfar)
        return carry

    lax.fori_loop(0, i - 1, far_body, 0)

    @pl.when(i > 0)
    def _():
        block(i - 1, bias_ref[1])

    block(i, bias_ref[0])

    lp = lam_ref[...]
    lam = (jnp.exp(jnp.sum(lp[0:1] * lp[1:2], axis=1, keepdims=True))
           - jnp.exp(jnp.sum(lp[2:3] * lp[3:4], axis=1, keepdims=True)) + lam_init)
    o = acc1[...] / l1[...] - lam * (acc2[...] / l2[...])
    ms = jnp.mean(o * o, axis=1, keepdims=True)
    y = o * lax.rsqrt(ms + NORM_EPS) * subw_ref[...] * (1.0 - lam_init)
    o_ref[...] = (y * _silu(g_ref[...].astype(F32))).astype(o_ref.dtype)


def _diff_attention(p3, lam_params, subln_w, bias_a, lam_init, tq):
    b, s, _ = p3.shape
    nq = s // tq
    dist = jnp.arange(tq)[:, None] - jnp.arange(tq)[None, :]
    vec = bias_a[_t5_bucket(jnp.arange(2 * tq)), :].astype(F32)
    diag = jnp.where((dist >= 0)[..., None], vec[jnp.maximum(dist, 0)], NEG)
    prev = vec[dist + tq]
    bias = jnp.stack([diag, prev], axis=0).transpose(3, 0, 1, 2)
    cfar = vec[2 * tq - 1]

    kernel = functools.partial(_diffattn_kernel, tq=tq, lam_init=lam_init)
    return pl.pallas_call(
        kernel,
        grid=(b, A_HEADS, nq),
        in_specs=[
            pl.BlockSpec(memory_space=pltpu.SMEM),
            pl.BlockSpec((4, A_QK), lambda bi, h, i: (0, 0)),
            pl.BlockSpec((None, tq, LANES), lambda bi, h, i: (bi, i, h)),
            pl.BlockSpec((None, s, LANES), lambda bi, h, i: (bi, 0, A_HEADS + h)),
            pl.BlockSpec((None, s, LANES), lambda bi, h, i: (bi, 0, 2 * A_HEADS + h)),
            pl.BlockSpec((None, tq, LANES), lambda bi, h, i: (bi, i, 3 * A_HEADS + h)),
            pl.BlockSpec((None, 2, tq, tq), lambda bi, h, i: (h, 0, 0, 0)),
            pl.BlockSpec((1, A_V), lambda bi, h, i: (0, 0)),
        ],
        out_specs=pl.BlockSpec((None, tq, LANES), lambda bi, h, i: (bi, i, h)),
        out_shape=jax.ShapeDtypeStruct((b, s, BRANCH_WIDTH), BF16),
        scratch_shapes=[pltpu.VMEM((tq, LANES), F32)] * 6,
        compiler_params=_cparams("parallel", "parallel", "parallel"),
        name="diff_attention",
    )(cfar, lam_params, p3, p3, p3, p3, bias, subln_w.reshape(1, A_V))


def _swa_kernel(sink_ref, q_ref, g_ref, kc_ref, kp_ref, vc_ref, vp_ref, bias_ref, o_ref):
    n = pl.program_id(1)
    scale = B_HD ** -0.5
    prev_pen = jnp.where(n > 0, 0.0, NEG).astype(F32)
    kc = kc_ref[...]
    kp = kp_ref[...]
    vc = vc_ref[...]
    vp = vp_ref[...]
    q = q_ref[...]
    gate = g_ref[...].astype(F32)
    dn = (((1,), (1,)), ((), ()))
    group = B_HEADS // B_KV
    for pair in range(B_HEADS // 2):
        outs = []
        for hq in (2 * pair, 2 * pair + 1):
            kv = hq // group
            qh = q[:, hq * B_HD:(hq + 1) * B_HD]
            ks = slice(kv * B_HD, (kv + 1) * B_HD)
            sp = (lax.dot_general(qh, kp[:, ks], dn, preferred_element_type=F32) * scale
                  + bias_ref[hq, 0] + prev_pen)
            sc = (lax.dot_general(qh, kc[:, ks], dn, preferred_element_type=F32) * scale
                  + bias_ref[hq, 1])
            sink = sink_ref[hq]
            m = jnp.maximum(jnp.max(sp, axis=1, keepdims=True), jnp.max(sc, axis=1, keepdims=True))
            m = jnp.maximum(m, sink)
            ep = jnp.exp(sp - m)
            ec = jnp.exp(sc - m)
            l = (jnp.sum(ep, axis=1, keepdims=True) + jnp.sum(ec, axis=1, keepdims=True)
                 + jnp.exp(sink - m))
            o = (jnp.dot(ep.astype(BF16), vp[:, ks], preferred_element_type=F32)
                 + jnp.dot(ec.astype(BF16), vc[:, ks], preferred_element_type=F32))
            outs.append(o / l)
        cols = slice(pair * 2 * B_HD, (pair + 1) * 2 * B_HD)
        o2 = jnp.concatenate(outs, axis=1)
        o_ref[:, cols] = (o2 * _silu(gate[:, cols])).astype(o_ref.dtype)


def _swa(p3, sinks, bias_b):
    b, s, _ = p3.shape
    nb = s // WINDOW
    qi = jnp.arange(WINDOW)[:, None]
    kj = jnp.arange(2 * WINDOW)[None, :]
    dist = WINDOW + qi - kj
    valid = (dist >= 0) & (dist < WINDOW)
    bias = bias_b[_t5_bucket(dist)].astype(F32)
    bias = jnp.where(valid[..., None], bias, NEG).transpose(2, 0, 1)
    bias = bias.reshape(B_HEADS, WINDOW, 2, WINDOW).transpose(0, 2, 1, 3)

    w = WINDOW
    prev_idx = lambda bi, n, c: (bi, jnp.maximum(n - 1, 0), c)
    return pl.pallas_call(
        _swa_kernel,
        grid=(b, nb),
        in_specs=[
            pl.BlockSpec(memory_space=pltpu.SMEM),
            pl.BlockSpec((None, w, 1024), lambda bi, n: (bi, n, COL_BQ // 1024)),
            pl.BlockSpec((None, w, 1024), lambda bi, n: (bi, n, COL_BG // 1024)),
            pl.BlockSpec((None, w, 256), lambda bi, n: (bi, n, COL_BK // 256)),
            pl.BlockSpec((None, w, 256), lambda bi, n: prev_idx(bi, n, COL_BK // 256)),
            pl.BlockSpec((None, w, 256), lambda bi, n: (bi, n, COL_BV // 256)),
            pl.BlockSpec((None, w, 256), lambda bi, n: prev_idx(bi, n, COL_BV // 256)),
            pl.BlockSpec((B_HEADS, 2, w, w), lambda bi, n: (0, 0, 0, 0)),
        ],
        out_specs=pl.BlockSpec((None, w, 1024), lambda bi, n: (bi, n, 0)),
        out_shape=jax.ShapeDtypeStruct((b, s, BRANCH_WIDTH), BF16),
        compiler_params=_cparams("parallel", "parallel"),
        name="swa",
    )(sinks, p3, p3, p3, p3, p3, p3, bias)


def _exact_dot01(v, r):
    hi = v.astype(BF16)
    r1 = v - hi.astype(F32)
    mid = r1.astype(BF16)
    lo = (r1 - mid.astype(F32)).astype(BF16)
    return (jnp.dot(hi, r, preferred_element_type=F32)
            + jnp.dot(mid, r, preferred_element_type=F32)
            + jnp.dot(lo, r, preferred_element_type=F32))


def _ssd_kernel(xbc_ref, z_ref, dt_ref, cw_ref, cb_ref, dtb_ref, alog_ref, dexp_ref, nw_ref,
                r64_ref, r128_ref, o_ref, ext_ref, state_ref):
    c = pl.program_id(1)
    halo = 8

    @pl.when(c == 0)
    def _():
        ext_ref[0:halo, :] = jnp.zeros((halo, C_XBC), F32)
        state_ref[...] = jnp.zeros(state_ref.shape, F32)

    xraw = xbc_ref[...].astype(F32)
    ext_ref[halo:halo + CHUNK, :] = xraw
    acc = jnp.broadcast_to(cb_ref[...], (CHUNK, C_XBC))
    for j in range(C_CONV):
        off = halo - (C_CONV - 1) + j
        acc = acc + cw_ref[j:j + 1, :] * ext_ref[off:off + CHUNK, :]
    ext_ref[0:halo, :] = xraw[CHUNK - halo:, :]
    xc = _silu(acc)
    xs = xc[:, :1024]
    bm = xc[:, 1024:1024 + C_GROUPS * C_STATE]
    cm = xc[:, 1024 + C_GROUPS * C_STATE:]

    dtx = dt_ref[...] + dtb_ref[...]
    dt = jnp.maximum(dtx, 0.0) + jnp.log1p(jnp.exp(-jnp.abs(dtx)))
    da = dt * (-jnp.exp(alog_ref[...]))
    row = lax.broadcasted_iota(jnp.int32, (CHUNK, CHUNK), 0)
    col = lax.broadcasted_iota(jnp.int32, (CHUNK, CHUNK), 1)
    causal = row >= col
    tri = causal.astype(BF16)
    hi = da.astype(BF16)
    r1 = da - hi.astype(F32)
    mid = r1.astype(BF16)
    lo = (r1 - mid.astype(F32)).astype(BF16)
    a_cum = (jnp.dot(tri, hi, preferred_element_type=F32)
             + jnp.dot(tri, mid, preferred_element_type=F32)
             + jnp.dot(tri, lo, preferred_element_type=F32))
    a_cum_t = a_cum.T

    r64 = r64_ref[...]
    dt_e = _exact_dot01(dt, r64)
    acum_e = _exact_dot01(a_cum, r64)
    acum_col = _exact_dot01(a_cum, r128_ref[...])
    alast_e = acum_e[CHUNK - 1:CHUNK, :]

    xdt = xs * dt_e
    xdt_b = xdt.astype(BF16)
    xw_b = (xdt * jnp.exp(alast_e - acum_e)).astype(BF16)
    chunk_decay = jnp.exp(alast_e)
    bm_b = bm.astype(BF16)
    cm_b = cm.astype(BF16)
    dn_nt = (((1,), (1,)), ((), ()))
    heads_per_group = C_HEADS // C_GROUPS
    gw = heads_per_group * C_HD

    y_parts = []
    for g in range(C_GROUPS):
        bg = bm[:, g * C_STATE:(g + 1) * C_STATE]
        bg_b = bm_b[:, g * C_STATE:(g + 1) * C_STATE]
        cg_b = cm_b[:, g * C_STATE:(g + 1) * C_STATE]
        cb = lax.dot_general(cg_b, bg_b, dn_nt, preferred_element_type=F32)
        bgt_b = bg.T.astype(BF16)
        gcols = slice(g * gw, (g + 1) * gw)
        new_states = jnp.dot(bgt_b, xw_b[:, gcols], preferred_element_type=F32)
        prev = state_ref[g]
        y_off = jnp.dot(cg_b, prev.astype(BF16), preferred_element_type=F32)
        state_ref[g] = prev * chunk_decay[:, gcols] + new_states
        diag = []
        for e in range(heads_per_group):
            hh = g * heads_per_group + e
            seg = acum_col[:, hh * LANES:(hh + 1) * LANES] - a_cum_t[hh:hh + 1, :]
            decay = jnp.exp(jnp.where(causal, seg, NEG))
            mh = (cb * decay).astype(BF16)
            diag.append(jnp.dot(mh, xdt_b[:, hh * C_HD:(hh + 1) * C_HD],
                                preferred_element_type=F32))
        y_diag = jnp.concatenate(diag, axis=1)
        y_g = (y_diag + y_off * jnp.exp(acum_e[:, gcols])
               + xs[:, gcols] * dexp_ref[:, gcols])
        zg = z_ref[:, gcols].astype(F32)
        yg = y_g * _silu(zg)
        ms = jnp.mean(yg * yg, axis=1, keepdims=True)
        y_parts.append(yg * lax.rsqrt(ms + NORM_EPS) * nw_ref[:, gcols])
    o_ref[...] = jnp.concatenate(y_parts, axis=1).astype(o_ref.dtype)


def _ssd(p3, dt3, conv_w, conv_b, dt_bias, a_log, d_skip, norm_w):
    b, s, _ = p3.shape
    nc = s // CHUNK
    pad = LANES - C_HEADS
    dtb = jnp.pad(dt_bias, (0, pad)).reshape(1, LANES)
    alog = jnp.pad(a_log, (0, pad)).reshape(1, LANES)
    dexp = jnp.repeat(d_skip, C_HD).reshape(1, BRANCH_WIDTH)
    heads = np.arange(LANES)[:, None]
    r64 = jnp.asarray(heads == (np.arange(C_HEADS * C_HD)[None, :] // C_HD), BF16)
    r128 = jnp.asarray(heads == (np.arange(C_HEADS * LANES)[None, :] // LANES), BF16)
    const = lambda bi, c: (0, 0)
    return pl.pallas_call(
        _ssd_kernel,
        grid=(b, nc),
        in_specs=[
            pl.BlockSpec((None, CHUNK, C_XBC), lambda bi, c: (bi, c, COL_CXBC // C_XBC)),
            pl.BlockSpec((None, CHUNK, 1024), lambda bi, c: (bi, c, COL_CZ // 1024)),
            pl.BlockSpec((None, CHUNK, LANES), lambda bi, c: (bi, c, 0)),
            pl.BlockSpec((C_CONV, C_XBC), const),
            pl.BlockSpec((1, C_XBC), const),
            pl.BlockSpec((1, LANES), const),
            pl.BlockSpec((1, LANES), const),
            pl.BlockSpec((1, BRANCH_WIDTH), const),
            pl.BlockSpec((1, BRANCH_WIDTH), const),
            pl.BlockSpec((LANES, C_HEADS * C_HD), const),
            pl.BlockSpec((LANES, C_HEADS * LANES), const),
        ],
        out_specs=pl.BlockSpec((None, CHUNK, BRANCH_WIDTH), lambda bi, c: (bi, c, 0)),
        out_shape=jax.ShapeDtypeStruct((b, s, BRANCH_WIDTH), BF16),
        scratch_shapes=[pltpu.VMEM((CHUNK + 8, C_XBC), F32),
                        pltpu.VMEM((C_GROUPS, C_STATE, 512), F32)],
        compiler_params=_cparams("parallel", "arbitrary"),
        name="ssd",
    )(p3, p3, dt3, conv_w, conv_b.reshape(1, C_XBC), dtb, alog, dexp,
      norm_w.reshape(1, BRANCH_WIDTH), r64, r128)


CONF_HALO = 32


def _conformer_kernel(glu_ref, g_ref, cw_ref, cb_ref, lnw_ref, lnb_ref, o_ref, ext_ref, *, ts):
    t = pl.program_id(1)

    @pl.when(t == 0)
    def _():
        ext_ref[0:CONF_HALO, :] = jnp.zeros((CONF_HALO, BRANCH_WIDTH), F32)

    glu = glu_ref[...].astype(F32)
    hcur = glu[:, :BRANCH_WIDTH] * _sigmoid(glu[:, BRANCH_WIDTH:])
    ext_ref[CONF_HALO:CONF_HALO + ts, :] = hcur
    acc = jnp.broadcast_to(cb_ref[...], (ts, BRANCH_WIDTH))
    for j in range(D_CONV):
        off = CONF_HALO - (D_CONV - 1) + j
        acc = acc + cw_ref[j:j + 1, :] * ext_ref[off:off + ts, :]
    ext_ref[0:CONF_HALO, :] = hcur[ts - CONF_HALO:, :]
    mu = jnp.mean(acc, axis=1, keepdims=True)
    xc = acc - mu
    var = jnp.mean(xc * xc, axis=1, keepdims=True)
    y = xc * lax.rsqrt(var + NORM_EPS) * lnw_ref[...] + lnb_ref[...]
    o_ref[...] = (_silu(y) * _silu(g_ref[...].astype(F32))).astype(o_ref.dtype)


def _conformer(p3, conv_w, conv_b, ln_w, ln_b, ts):
    b, s, _ = p3.shape
    ts = min(ts, s)
    const = lambda bi, t: (0, 0)
    w = BRANCH_WIDTH
    return pl.pallas_call(
        functools.partial(_conformer_kernel, ts=ts),
        grid=(b, s // ts),
        in_specs=[
            pl.BlockSpec((None, ts, 2 * w), lambda bi, t: (bi, t, COL_DGLU // (2 * w))),
            pl.BlockSpec((None, ts, w), lambda bi, t: (bi, t, COL_DG // w)),
            pl.BlockSpec((D_CONV, w), const),
            pl.BlockSpec((1, w), const),
            pl.BlockSpec((1, w), const),
            pl.BlockSpec((1, w), const),
        ],
        out_specs=pl.BlockSpec((None, ts, w), lambda bi, t: (bi, t, 0)),
        out_shape=jax.ShapeDtypeStruct((b, s, w), BF16),
        scratch_shapes=[pltpu.VMEM((CONF_HALO + ts, w), F32)],
        compiler_params=_cparams("parallel", "arbitrary"),
        name="conformer",
    )(p3, p3, conv_w, conv_b.reshape(1, w), ln_w.reshape(1, w), ln_b.reshape(1, w))


def _merge_kernel(h_ref, ya_ref, yb_ref, yc_ref, yd_ref, wg0, wg1, wg2, wg3, wb_ref, o_ref):
    h = h_ref[...]
    acc = None
    for i, (y_ref, wg_ref) in enumerate(((ya_ref, wg0), (yb_ref, wg1), (yc_ref, wg2), (yd_ref, wg3))):
        gate = jnp.dot(h, wg_ref[...], preferred_element_type=F32)
        up = jnp.dot(y_ref[...], wb_ref[i], preferred_element_type=F32)
        term = _sigmoid(gate) * up
        acc = term if acc is None else acc + term
    o_ref[...] = acc.astype(o_ref.dtype)


def _merge(h2, ys, wg, wb, tm, tn):
    t, d = h2.shape
    tm = min(tm, t)
    nblk = d // tn
    y_spec = pl.BlockSpec((tm, BRANCH_WIDTH), lambda i, j: (i, 0))
    wg_specs = [pl.BlockSpec((d, tn), functools.partial(lambda i, j, k: (0, k * nblk + j), k=k))
                for k in range(N_BRANCH)]
    return pl.pallas_call(
        _merge_kernel,
        grid=(t // tm, nblk),
        in_specs=[pl.BlockSpec((tm, d), lambda i, j: (i, 0)), y_spec, y_spec, y_spec, y_spec,
                  *wg_specs,
                  pl.BlockSpec((N_BRANCH, BRANCH_WIDTH, tn), lambda i, j: (0, 0, j))],
        out_specs=pl.BlockSpec((tm, tn), lambda i, j: (i, j)),
        out_shape=jax.ShapeDtypeStruct((t, d), BF16),
        compiler_params=_cparams("parallel", "parallel"),
        name="merge",
    )(h2, *ys, wg, wg, wg, wg, wb)


def _outproj_kernel(m_ref, w_ref, x_ref, nw_ref, *out_refs, last):
    xn = x_ref[...] + jnp.dot(m_ref[...], w_ref[...], preferred_element_type=F32)
    ms = jnp.mean(xn * xn, axis=-1, keepdims=True)
    normed = xn * lax.rsqrt(ms + NORM_EPS) * nw_ref[...]
    if last:
        out_refs[0][...] = normed
    else:
        out_refs[0][...] = xn
        out_refs[1][...] = normed.astype(BF16)


def _outproj(merged, w_out, x2, next_norm_w, last, tm):
    t, d = x2.shape
    tm = min(tm, t)
    row = pl.BlockSpec((tm, d), lambda i: (i, 0))
    if last:
        out_specs = row
        out_shape = jax.ShapeDtypeStruct((t, d), F32)
    else:
        out_specs = (row, row)
        out_shape = (jax.ShapeDtypeStruct((t, d), F32), jax.ShapeDtypeStruct((t, d), BF16))
    return pl.pallas_call(
        functools.partial(_outproj_kernel, last=last),
        grid=(t // tm,),
        in_specs=[row, pl.BlockSpec((d, d), lambda i: (0, 0)), row,
                  pl.BlockSpec((1, d), lambda i: (0, 0))],
        out_specs=out_specs,
        out_shape=out_shape,
        compiler_params=_cparams("parallel"),
        name="outproj",
    )(merged, w_out, x2, next_norm_w.reshape(1, d))


def kernel(x, norm_w, w_in, diff_lambda, diff_subln_w, swa_sinks, ssd_conv_w, ssd_conv_b,
           ssd_dt_bias, ssd_a_log, ssd_d, ssd_norm_w, conf_conv_w, conf_conv_b, conf_ln_w,
           conf_ln_b, w_branch, w_out, rel_bias, final_norm_w):
    b, s, d = x.shape
    depth = w_in.shape[0]
    t = b * s
    bias_a = rel_bias[:, :A_HEADS]
    bias_b = rel_bias[:, A_HEADS:]
    tq = min(256, s)

    x2 = x.reshape(t, d)
    h2 = _rmsnorm(x2, norm_w[0], BF16)
    out = None
    for l in range(depth):
        wl = w_in[l]
        w_mix = jnp.concatenate([wl[:, a:z] for a, z in _SRC_RANGES], axis=1).astype(BF16)
        w_dt = jnp.pad(wl[:, SRC_DT[0]:SRC_DT[1]], ((0, 0), (0, LANES - C_HEADS))).astype(BF16)
        w_gate = wl[:, SRC_MG:].astype(BF16)
        w_br = w_branch[l].astype(BF16)
        w_o = w_out[l].astype(BF16)

        proj = _matmul(h2, w_mix, BF16, 1024, 1024, "inproj")
        dt_raw = _matmul(h2, w_dt, F32, 1024, LANES, "dtproj")
        p3 = proj.reshape(b, s, N_MIX)
        dt3 = dt_raw.reshape(b, s, LANES)

        lam_init = 0.8 - 0.6 * math.exp(-0.3 * l)
        ya = _diff_attention(p3, diff_lambda[l], diff_subln_w[l], bias_a, lam_init, tq)
        yb = _swa(p3, swa_sinks[l], bias_b)
        yc = _ssd(p3, dt3, ssd_conv_w[l], ssd_conv_b[l], ssd_dt_bias[l], ssd_a_log[l],
                  ssd_d[l], ssd_norm_w[l])
        yd = _conformer(p3, conf_conv_w[l], conf_conv_b[l], conf_ln_w[l], conf_ln_b[l], 256)

        ys = [y.reshape(t, BRANCH_WIDTH) for y in (ya, yb, yc, yd)]
        merged = _merge(h2, ys, w_gate, w_br, 1024, 256)
        last = l == depth - 1
        next_w = final_norm_w if last else norm_w[l + 1]
        res = _outproj(merged, w_o, x2, next_w, last, 512)
        if last:
            out = res
        else:
            x2, h2 = res
    return out.reshape(b, s, d)
```

```python
import functools
import math

import jax
import jax.numpy as jnp
import numpy as np
from jax import lax
from jax.experimental import pallas as pl
from jax.experimental.pallas import tpu as pltpu

F32 = jnp.float32
BF16 = jnp.bfloat16

D_MODEL = 2048
NORM_EPS = 1e-6
BRANCH_WIDTH = 1024
NUM_BUCKETS = 32
MAX_DISTANCE = 128

A_QK = 64
A_V = 128
A_HEADS = 8
B_HD = 64
B_HEADS = 16
B_KV = 4
WINDOW = 128
C_HEADS = 16
C_HD = 64
C_GROUPS = 2
C_STATE = 128
C_CONV = 4
CHUNK = 128
C_XBC = 1536
D_CONV = 31
N_BRANCH = 4

COL_A = 0
COL_BQ = 4096
COL_BG = 5120
COL_CXBC = 6144
COL_BK = 7680
COL_BV = 7936
COL_CZ = 8192
COL_DG = 9216
COL_DGLU = 10240
N_MIX = 12288
_SRC_RANGES = (
    (0, 4096),
    (4096, 5120),
    (5632, 6656),
    (6656, 8192),
    (5120, 5376),
    (5376, 5632),
    (8208, 9232),
    (11280, 12304),
    (9232, 11280),
)
SRC_DT = (8192, 8208)
SRC_MG = 12304

NEG = -1e30
LANES = 128
VMEM_LIMIT = 56 * 1024 * 1024


def _cparams(*sem):
    return pltpu.CompilerParams(dimension_semantics=sem, vmem_limit_bytes=VMEM_LIMIT)


def _sigmoid(x):
    return 1.0 / (1.0 + jnp.exp(-x))


def _silu(x):
    return x * _sigmoid(x)


def _rmsnorm_kernel(x_ref, w_ref, o_ref):
    x = x_ref[...]
    ms = jnp.mean(x * x, axis=-1, keepdims=True)
    o_ref[...] = (x * lax.rsqrt(ms + NORM_EPS) * w_ref[...]).astype(o_ref.dtype)


def _rmsnorm(x2, w, out_dtype):
    t, d = x2.shape
    tm = min(512, t)
    return pl.pallas_call(
        _rmsnorm_kernel,
        grid=(t // tm,),
        in_specs=[pl.BlockSpec((tm, d), lambda i: (i, 0)),
                  pl.BlockSpec((1, d), lambda i: (0, 0))],
        out_specs=pl.BlockSpec((tm, d), lambda i: (i, 0)),
        out_shape=jax.ShapeDtypeStruct((t, d), out_dtype),
        compiler_params=_cparams("parallel"),
        name="rmsnorm",
    )(x2, w.reshape(1, d))


def _matmul_kernel(a_ref, b_ref, o_ref):
    o_ref[...] = jnp.dot(a_ref[...], b_ref[...],
                         preferred_element_type=F32).astype(o_ref.dtype)


def _matmul(a, b, out_dtype, tm, tn, name):
    m, k = a.shape
    n = b.shape[1]
    tm = min(tm, m)
    tn = min(tn, n)
    return pl.pallas_call(
        _matmul_kernel,
        grid=(m // tm, n // tn),
        in_specs=[pl.BlockSpec((tm, k), lambda i, j: (i, 0)),
                  pl.BlockSpec((k, tn), lambda i, j: (0, j))],
        out_specs=pl.BlockSpec((tm, tn), lambda i, j: (i, j)),
        out_shape=jax.ShapeDtypeStruct((m, n), out_dtype),
        compiler_params=_cparams("parallel", "parallel"),
        name=name,
    )(a, b)


def _bucket_lower_bounds():
    max_exact = NUM_BUCKETS // 2
    lower = list(range(max_exact))
    for k in range(max_exact, NUM_BUCKETS):
        d = max_exact
        while True:
            large = max_exact + int(math.log(d / max_exact) / math.log(MAX_DISTANCE / max_exact)
                                    * (NUM_BUCKETS - max_exact))
            if min(large, NUM_BUCKETS - 1) >= k:
                break
            d += 1
        lower.append(d)
    return tuple(lower)


BUCKET_LOWER = _bucket_lower_bounds()


def _bias_of_distance(d, tbl_ref, h):
    val = jnp.zeros(d.shape, F32) + tbl_ref[0, h]
    for k in range(1, NUM_BUCKETS):
        val = jnp.where(d >= BUCKET_LOWER[k], tbl_ref[k, h], val)
    return val


def _bias_tiles_kernel(tbl_ref, o_ref, *, n, head0, window):
    h = pl.program_id(0) + head0
    row = lax.broadcasted_iota(jnp.int32, (n, n), 0)
    col = lax.broadcasted_iota(jnp.int32, (n, n), 1)
    d_prev = row - col + n
    d_cur = row - col
    prev = _bias_of_distance(d_prev, tbl_ref, h)
    if window is not None:
        prev = jnp.where(d_prev < window, prev, NEG)
    o_ref[0] = prev
    o_ref[1] = jnp.where(d_cur >= 0, _bias_of_distance(jnp.maximum(d_cur, 0), tbl_ref, h), NEG)


def _bias_tiles(rel_bias, n, head0, heads, window):
    return pl.pallas_call(
        functools.partial(_bias_tiles_kernel, n=n, head0=head0, window=window),
        grid=(heads,),
        in_specs=[pl.BlockSpec(memory_space=pltpu.SMEM)],
        out_specs=pl.BlockSpec((None, 2, n, n), lambda h: (h, 0, 0, 0)),
        out_shape=jax.ShapeDtypeStruct((heads, 2, n, n), F32),
        compiler_params=_cparams("parallel"),
        name="bias_tiles",
    )(rel_bias)


def _online_update(s, v_blk, m_ref, l_ref, acc_ref, reps):
    m_prev = m_ref[...]
    m_next = jnp.maximum(m_prev, jnp.max(s, axis=1, keepdims=True))
    p = jnp.exp(s - jnp.concatenate([m_next] * reps, axis=1))
    alpha = jnp.exp(m_prev - m_next)
    l_ref[...] = alpha * l_ref[...] + jnp.sum(p, axis=1, keepdims=True)
    acc_ref[...] = alpha * acc_ref[...] + jnp.dot(
        p.astype(BF16), v_blk, preferred_element_type=F32)
    m_ref[...] = m_next


def _diffattn_kernel(tbl_ref, lam_ref, q_ref, k_ref, v_ref, g_ref, bias_ref, subw_ref,
                     o_ref, m1, l1, acc1, m2, l2, acc2, *, tq, lam_init):
    h = pl.program_id(1)
    i = pl.program_id(2)
    reps = tq // LANES
    scale = A_QK ** -0.5

    for m_ref, l_ref, acc_ref in ((m1, l1, acc1), (m2, l2, acc2)):
        m_ref[...] = jnp.full(m_ref.shape, NEG, F32)
        l_ref[...] = jnp.zeros(l_ref.shape, F32)
        acc_ref[...] = jnp.zeros(acc_ref.shape, F32)

    q = q_ref[...]
    q1 = q[:, :A_QK]
    q2 = q[:, A_QK:]

    def block(j, bias):
        start = pl.multiple_of(j * tq, tq)
        kj = k_ref[pl.ds(start, tq), :]
        vj = v_ref[pl.ds(start, tq), :]
        dn = (((1,), (1,)), ((), ()))
        s1 = lax.dot_general(q1, kj[:, :A_QK], dn, preferred_element_type=F32) * scale + bias
        s2 = lax.dot_general(q2, kj[:, A_QK:], dn, preferred_element_type=F32) * scale + bias
        _online_update(s1, vj, m1, l1, acc1, reps)
        _online_update(s2, vj, m2, l2, acc2, reps)

    cfar = tbl_ref[NUM_BUCKETS - 1, h]

    def far_body(j, carry):
        block(j, cfar)
        return carry

    lax.fori_loop(0, i - 1, far_body, 0)

    @pl.when(i > 0)
    def _():
        block(i - 1, bias_ref[0])

    block(i, bias_ref[1])

    lp = lam_ref[...]
    lam = (jnp.exp(jnp.sum(lp[0:1] * lp[1:2], axis=1, keepdims=True))
           - jnp.exp(jnp.sum(lp[2:3] * lp[3:4], axis=1, keepdims=True)) + lam_init)
    o = acc1[...] / l1[...] - lam * (acc2[...] / l2[...])
    ms = jnp.mean(o * o, axis=1, keepdims=True)
    y = o * lax.rsqrt(ms + NORM_EPS) * subw_ref[...] * (1.0 - lam_init)
    o_ref[...] = (y * _silu(g_ref[...].astype(F32))).astype(o_ref.dtype)


def _diff_attention(p3, lam_params, subln_w, rel_bias, bias, lam_init, tq):
    b, s, _ = p3.shape
    nq = s // tq
    assert tq + 1 >= BUCKET_LOWER[-1]

    kernel = functools.partial(_diffattn_kernel, tq=tq, lam_init=lam_init)
    return pl.pallas_call(
        kernel,
        grid=(b, A_HEADS, nq),
        in_specs=[
            pl.BlockSpec(memory_space=pltpu.SMEM),
            pl.BlockSpec((4, A_QK), lambda bi, h, i: (0, 0)),
            pl.BlockSpec((None, tq, LANES), lambda bi, h, i: (bi, i, h)),
            pl.BlockSpec((None, s, LANES), lambda bi, h, i: (bi, 0, A_HEADS + h)),
            pl.BlockSpec((None, s, LANES), lambda bi, h, i: (bi, 0, 2 * A_HEADS + h)),
            pl.BlockSpec((None, tq, LANES), lambda bi, h, i: (bi, i, 3 * A_HEADS + h)),
            pl.BlockSpec((None, 2, tq, tq), lambda bi, h, i: (h, 0, 0, 0)),
            pl.BlockSpec((1, A_V), lambda bi, h, i: (0, 0)),
        ],
        out_specs=pl.BlockSpec((None, tq, LANES), lambda bi, h, i: (bi, i, h)),
        out_shape=jax.ShapeDtypeStruct((b, s, BRANCH_WIDTH), BF16),
        scratch_shapes=[pltpu.VMEM((tq, LANES), F32)] * 6,
        compiler_params=_cparams("parallel", "parallel", "parallel"),
        name="diff_attention",
    )(rel_bias, lam_params, p3, p3, p3, p3, bias, subln_w.reshape(1, A_V))


def _swa_kernel(sink_ref, q_ref, g_ref, kc_ref, kp_ref, vc_ref, vp_ref, bias_ref, o_ref):
    n = pl.program_id(1)
    scale = B_HD ** -0.5
    prev_pen = jnp.where(n > 0, 0.0, NEG).astype(F32)
    kc = kc_ref[...]
    kp = kp_ref[...]
    vc = vc_ref[...]
    vp = vp_ref[...]
    q = q_ref[...]
    gate = g_ref[...].astype(F32)
    dn = (((1,), (1,)), ((), ()))
    group = B_HEADS // B_KV
    for pair in range(B_HEADS // 2):
        outs = []
        for hq in (2 * pair, 2 * pair + 1):
            kv = hq // group
            qh = q[:, hq * B_HD:(hq + 1) * B_HD]
            ks = slice(kv * B_HD, (kv + 1) * B_HD)
            sp = (lax.dot_general(qh, kp[:, ks], dn, preferred_element_type=F32) * scale
                  + bias_ref[hq, 0] + prev_pen)
            sc = (lax.dot_general(qh, kc[:, ks], dn, preferred_element_type=F32) * scale
                  + bias_ref[hq, 1])
            sink = sink_ref[hq]
            m = jnp.maximum(jnp.max(sp, axis=1, keepdims=True), jnp.max(sc, axis=1, keepdims=True))
            m = jnp.maximum(m, sink)
            ep = jnp.exp(sp - m)
            ec = jnp.exp(sc - m)
            l = (jnp.sum(ep, axis=1, keepdims=True) + jnp.sum(ec, axis=1, keepdims=True)
                 + jnp.exp(sink - m))
            o = (jnp.dot(ep.astype(BF16), vp[:, ks], preferred_element_type=F32)
                 + jnp.dot(ec.astype(BF16), vc[:, ks], preferred_element_type=F32))
            outs.append(o / l)
        cols = slice(pair * 2 * B_HD, (pair + 1) * 2 * B_HD)
        o2 = jnp.concatenate(outs, axis=1)
        o_ref[:, cols] = (o2 * _silu(gate[:, cols])).astype(o_ref.dtype)


def _swa(p3, sinks, bias):
    b, s, _ = p3.shape
    nb = s // WINDOW

    w = WINDOW
    prev_idx = lambda bi, n, c: (bi, jnp.maximum(n - 1, 0), c)
    return pl.pallas_call(
        _swa_kernel,
        grid=(b, nb),
        in_specs=[
            pl.BlockSpec(memory_space=pltpu.SMEM),
            pl.BlockSpec((None, w, 1024), lambda bi, n: (bi, n, COL_BQ // 1024)),
            pl.BlockSpec((None, w, 1024), lambda bi, n: (bi, n, COL_BG // 1024)),
            pl.BlockSpec((None, w, 256), lambda bi, n: (bi, n, COL_BK // 256)),
            pl.BlockSpec((None, w, 256), lambda bi, n: prev_idx(bi, n, COL_BK // 256)),
            pl.BlockSpec((None, w, 256), lambda bi, n: (bi, n, COL_BV // 256)),
            pl.BlockSpec((None, w, 256), lambda bi, n: prev_idx(bi, n, COL_BV // 256)),
            pl.BlockSpec((B_HEADS, 2, w, w), lambda bi, n: (0, 0, 0, 0)),
        ],
        out_specs=pl.BlockSpec((None, w, 1024), lambda bi, n: (bi, n, 0)),
        out_shape=jax.ShapeDtypeStruct((b, s, BRANCH_WIDTH), BF16),
        compiler_params=_cparams("parallel", "parallel"),
        name="swa",
    )(sinks, p3, p3, p3, p3, p3, p3, bias)


def _exact_dot01(v, r):
    hi = v.astype(BF16)
    r1 = v - hi.astype(F32)
    mid = r1.astype(BF16)
    lo = (r1 - mid.astype(F32)).astype(BF16)
    return (jnp.dot(hi, r, preferred_element_type=F32)
            + jnp.dot(mid, r, preferred_element_type=F32)
            + jnp.dot(lo, r, preferred_element_type=F32))


def _ssd_kernel(xbc_ref, z_ref, dt_ref, cw_ref, cb_ref, dtb_ref, alog_ref, dexp_ref, nw_ref,
                r64_ref, r128_ref, o_ref, ext_ref, state_ref):
    c = pl.program_id(1)
    halo = 8

    @pl.when(c == 0)
    def _():
        ext_ref[0:halo, :] = jnp.zeros((halo, C_XBC), F32)
        state_ref[...] = jnp.zeros(state_ref.shape, F32)

    xraw = xbc_ref[...].astype(F32)
    ext_ref[halo:halo + CHUNK, :] = xraw
    acc = jnp.broadcast_to(cb_ref[...], (CHUNK, C_XBC))
    for j in range(C_CONV):
        off = halo - (C_CONV - 1) + j
        acc = acc + cw_ref[j:j + 1, :] * ext_ref[off:off + CHUNK, :]
    ext_ref[0:halo, :] = xraw[CHUNK - halo:, :]
    xc = _silu(acc)
    xs = xc[:, :1024]
    bm = xc[:, 1024:1024 + C_GROUPS * C_STATE]
    cm = xc[:, 1024 + C_GROUPS * C_STATE:]

    dtx = dt_ref[...] + dtb_ref[...]
    dt = jnp.maximum(dtx, 0.0) + jnp.log1p(jnp.exp(-jnp.abs(dtx)))
    da = dt * (-jnp.exp(alog_ref[...]))
    row = lax.broadcasted_iota(jnp.int32, (CHUNK, CHUNK), 0)
    col = lax.broadcasted_iota(jnp.int32, (CHUNK, CHUNK), 1)
    causal = row >= col
    tri = causal.astype(BF16)
    hi = da.astype(BF16)
    r1 = da - hi.astype(F32)
    mid = r1.astype(BF16)
    lo = (r1 - mid.astype(F32)).astype(BF16)
    a_cum = (jnp.dot(tri, hi, preferred_element_type=F32)
             + jnp.dot(tri, mid, preferred_element_type=F32)
             + jnp.dot(tri, lo, preferred_element_type=F32))
    a_cum_t = a_cum.T

    r64 = r64_ref[...]
    dt_e = _exact_dot01(dt, r64)
    acum_e = _exact_dot01(a_cum, r64)
    acum_col = _exact_dot01(a_cum, r128_ref[...])
    alast_e = acum_e[CHUNK - 1:CHUNK, :]

    xdt = xs * dt_e
    xdt_b = xdt.astype(BF16)
    xw_b = (xdt * jnp.exp(alast_e - acum_e)).astype(BF16)
    chunk_decay = jnp.exp(alast_e)
    bm_b = bm.astype(BF16)
    cm_b = cm.astype(BF16)
    dn_nt = (((1,), (1,)), ((), ()))
    heads_per_group = C_HEADS // C_GROUPS
    gw = heads_per_group * C_HD

    y_parts = []
    for g in range(C_GROUPS):
        bg = bm[:, g * C_STATE:(g + 1) * C_STATE]
        bg_b = bm_b[:, g * C_STATE:(g + 1) * C_STATE]
        cg_b = cm_b[:, g * C_STATE:(g + 1) * C_STATE]
        cb = lax.dot_general(cg_b, bg_b, dn_nt, preferred_element_type=F32)
        bgt_b = bg.T.astype(BF16)
        gcols = slice(g * gw, (g + 1) * gw)
        new_states = jnp.dot(bgt_b, xw_b[:, gcols], preferred_element_type=F32)
        prev = state_ref[g]
        y_off = jnp.dot(cg_b, prev.astype(BF16), preferred_element_type=F32)
        state_ref[g] = prev * chunk_decay[:, gcols] + new_states
        diag = []
        for e in range(heads_per_group):
            hh = g * heads_per_group + e
            seg = acum_col[:, hh * LANES:(hh + 1) * LANES] - a_cum_t[hh:hh + 1, :]
            decay = jnp.exp(jnp.where(causal, seg, NEG))
            mh = (cb * decay).astype(BF16)
            diag.append(jnp.dot(mh, xdt_b[:, hh * C_HD:(hh + 1) * C_HD],
                                preferred_element_type=F32))
        y_diag = jnp.concatenate(diag, axis=1)
        y_g = (y_diag + y_off * jnp.exp(acum_e[:, gcols])
               + xs[:, gcols] * dexp_ref[:, gcols])
        zg = z_ref[:, gcols].astype(F32)
        yg = y_g * _silu(zg)
        ms = jnp.mean(yg * yg, axis=1, keepdims=True)
        y_parts.append(yg * lax.rsqrt(ms + NORM_EPS) * nw_ref[:, gcols])
    o_ref[...] = jnp.concatenate(y_parts, axis=1).astype(o_ref.dtype)


def _ssd(p3, dt3, conv_w, conv_b, dt_bias, a_log, d_skip, norm_w):
    b, s, _ = p3.shape
    nc = s // CHUNK
    pad = LANES - C_HEADS
    dtb = jnp.pad(dt_bias, (0, pad)).reshape(1, LANES)
    alog = jnp.pad(a_log, (0, pad)).reshape(1, LANES)
    dexp = jnp.repeat(d_skip, C_HD).reshape(1, BRANCH_WIDTH)
    heads = np.arange(LANES)[:, None]
    r64 = jnp.asarray(heads == (np.arange(C_HEADS * C_HD)[None, :] // C_HD), BF16)
    r128 = jnp.asarray(heads == (np.arange(C_HEADS * LANES)[None, :] // LANES), BF16)
    const = lambda bi, c: (0, 0)
    return pl.pallas_call(
        _ssd_kernel,
        grid=(b, nc),
        in_specs=[
            pl.BlockSpec((None, CHUNK, C_XBC), lambda bi, c: (bi, c, COL_CXBC // C_XBC)),
            pl.BlockSpec((None, CHUNK, 1024), lambda bi, c: (bi, c, COL_CZ // 1024)),
            pl.BlockSpec((None, CHUNK, LANES), lambda bi, c: (bi, c, 0)),
            pl.BlockSpec((C_CONV, C_XBC), const),
            pl.BlockSpec((1, C_XBC), const),
            pl.BlockSpec((1, LANES), const),
            pl.BlockSpec((1, LANES), const),
            pl.BlockSpec((1, BRANCH_WIDTH), const),
            pl.BlockSpec((1, BRANCH_WIDTH), const),
            pl.BlockSpec((LANES, C_HEADS * C_HD), const),
            pl.BlockSpec((LANES, C_HEADS * LANES), const),
        ],
        out_specs=pl.BlockSpec((None, CHUNK, BRANCH_WIDTH), lambda bi, c: (bi, c, 0)),
        out_shape=jax.ShapeDtypeStruct((b, s, BRANCH_WIDTH), BF16),
        scratch_shapes=[pltpu.VMEM((CHUNK + 8, C_XBC), F32),
                        pltpu.VMEM((C_GROUPS, C_STATE, 512), F32)],
        compiler_params=_cparams("parallel", "arbitrary"),
        name="ssd",
    )(p3, p3, dt3, conv_w, conv_b.reshape(1, C_XBC), dtb, alog, dexp,
      norm_w.reshape(1, BRANCH_WIDTH), r64, r128)


CONF_HALO = 32


def _conformer_kernel(glu_ref, g_ref, cw_ref, cb_ref, lnw_ref, lnb_ref, o_ref, ext_ref, *, ts):
    t = pl.program_id(1)

    @pl.when(t == 0)
    def _():
        ext_ref[0:CONF_HALO, :] = jnp.zeros((CONF_HALO, BRANCH_WIDTH), F32)

    glu = glu_ref[...].astype(F32)
    hcur = glu[:, :BRANCH_WIDTH] * _sigmoid(glu[:, BRANCH_WIDTH:])
    ext_ref[CONF_HALO:CONF_HALO + ts, :] = hcur
    acc = jnp.broadcast_to(cb_ref[...], (ts, BRANCH_WIDTH))
    for j in range(D_CONV):
        off = CONF_HALO - (D_CONV - 1) + j
        acc = acc + cw_ref[j:j + 1, :] * ext_ref[off:off + ts, :]
    ext_ref[0:CONF_HALO, :] = hcur[ts - CONF_HALO:, :]
    mu = jnp.mean(acc, axis=1, keepdims=True)
    xc = acc - mu
    var = jnp.mean(xc * xc, axis=1, keepdims=True)
    y = xc * lax.rsqrt(var + NORM_EPS) * lnw_ref[...] + lnb_ref[...]
    o_ref[...] = (_silu(y) * _silu(g_ref[...].astype(F32))).astype(o_ref.dtype)


def _conformer(p3, conv_w, conv_b, ln_w, ln_b, ts):
    b, s, _ = p3.shape
    ts = min(ts, s)
    const = lambda bi, t: (0, 0)
    w = BRANCH_WIDTH
    return pl.pallas_call(
        functools.partial(_conformer_kernel, ts=ts),
        grid=(b, s // ts),
        in_specs=[
            pl.BlockSpec((None, ts, 2 * w), lambda bi, t: (bi, t, COL_DGLU // (2 * w))),
            pl.BlockSpec((None, ts, w), lambda bi, t: (bi, t, COL_DG // w)),
            pl.BlockSpec((D_CONV, w), const),
            pl.BlockSpec((1, w), const),
            pl.BlockSpec((1, w), const),
            pl.BlockSpec((1, w), const),
        ],
        out_specs=pl.BlockSpec((None, ts, w), lambda bi, t: (bi, t, 0)),
        out_shape=jax.ShapeDtypeStruct((b, s, w), BF16),
        scratch_shapes=[pltpu.VMEM((CONF_HALO + ts, w), F32)],
        compiler_params=_cparams("parallel", "arbitrary"),
        name="conformer",
    )(p3, p3, conv_w, conv_b.reshape(1, w), ln_w.reshape(1, w), ln_b.reshape(1, w))


def _merge_kernel(h_ref, ya_ref, yb_ref, yc_ref, yd_ref, wg0, wg1, wg2, wg3, wb_ref, o_ref):
    h = h_ref[...]
    acc = None
    for i, (y_ref, wg_ref) in enumerate(((ya_ref, wg0), (yb_ref, wg1), (yc_ref, wg2), (yd_ref, wg3))):
        gate = jnp.dot(h, wg_ref[...], preferred_element_type=F32)
        up = jnp.dot(y_ref[...], wb_ref[i], preferred_element_type=F32)
        term = _sigmoid(gate) * up
        acc = term if acc is None else acc + term
    o_ref[...] = acc.astype(o_ref.dtype)


def _merge(h2, ys, wg, wb, tm, tn):
    t, d = h2.shape
    tm = min(tm, t)
    nblk = d // tn
    y_spec = pl.BlockSpec((tm, BRANCH_WIDTH), lambda i, j: (i, 0))
    wg_specs = [pl.BlockSpec((d, tn), functools.partial(lambda i, j, k: (0, k * nblk + j), k=k))
                for k in range(N_BRANCH)]
    return pl.pallas_call(
        _merge_kernel,
        grid=(t // tm, nblk),
        in_specs=[pl.BlockSpec((tm, d), lambda i, j: (i, 0)), y_spec, y_spec, y_spec, y_spec,
                  *wg_specs,
                  pl.BlockSpec((N_BRANCH, BRANCH_WIDTH, tn), lambda i, j: (0, 0, j))],
        out_specs=pl.BlockSpec((tm, tn), lambda i, j: (i, j)),
        out_shape=jax.ShapeDtypeStruct((t, d), BF16),
        compiler_params=_cparams("parallel", "parallel"),
        name="merge",
    )(h2, *ys, wg, wg, wg, wg, wb)


def _outproj_kernel(m_ref, w_ref, x_ref, nw_ref, *out_refs, last):
    xn = x_ref[...] + jnp.dot(m_ref[...], w_ref[...], preferred_element_type=F32)
    ms = jnp.mean(xn * xn, axis=-1, keepdims=True)
    normed = xn * lax.rsqrt(ms + NORM_EPS) * nw_ref[...]
    if last:
        out_refs[0][...] = normed
    else:
        out_refs[0][...] = xn
        out_refs[1][...] = normed.astype(BF16)


def _outproj(merged, w_out, x2, next_norm_w, last, tm):
    t, d = x2.shape
    tm = min(tm, t)
    row = pl.BlockSpec((tm, d), lambda i: (i, 0))
    if last:
        out_specs = row
        out_shape = jax.ShapeDtypeStruct((t, d), F32)
    else:
        out_specs = (row, row)
        out_shape = (jax.ShapeDtypeStruct((t, d), F32), jax.ShapeDtypeStruct((t, d), BF16))
    return pl.pallas_call(
        functools.partial(_outproj_kernel, last=last),
        grid=(t // tm,),
        in_specs=[row, pl.BlockSpec((d, d), lambda i: (0, 0)), row,
                  pl.BlockSpec((1, d), lambda i: (0, 0))],
        out_specs=out_specs,
        out_shape=out_shape,
        compiler_params=_cparams("parallel"),
        name="outproj",
    )(merged, w_out, x2, next_norm_w.reshape(1, d))


def kernel(x, norm_w, w_in, diff_lambda, diff_subln_w, swa_sinks, ssd_conv_w, ssd_conv_b,
           ssd_dt_bias, ssd_a_log, ssd_d, ssd_norm_w, conf_conv_w, conf_conv_b, conf_ln_w,
           conf_ln_b, w_branch, w_out, rel_bias, final_norm_w):
    b, s, d = x.shape
    depth = w_in.shape[0]
    t = b * s
    tq = min(256, s)
    bias_a = _bias_tiles(rel_bias, tq, 0, A_HEADS, None)
    bias_b = _bias_tiles(rel_bias, WINDOW, A_HEADS, B_HEADS, WINDOW)

    x2 = x.reshape(t, d)
    h2 = _rmsnorm(x2, norm_w[0], BF16)
    out = None
    for l in range(depth):
        wl = w_in[l]
        w_mix = jnp.concatenate([wl[:, a:z] for a, z in _SRC_RANGES], axis=1).astype(BF16)
        w_dt = jnp.pad(wl[:, SRC_DT[0]:SRC_DT[1]], ((0, 0), (0, LANES - C_HEADS))).astype(BF16)
        w_gate = wl[:, SRC_MG:].astype(BF16)
        w_br = w_branch[l].astype(BF16)
        w_o = w_out[l].astype(BF16)

        proj = _matmul(h2, w_mix, BF16, 1024, 1024, "inproj")
        dt_raw = _matmul(h2, w_dt, F32, 1024, LANES, "dtproj")
        p3 = proj.reshape(b, s, N_MIX)
        dt3 = dt_raw.reshape(b, s, LANES)

        lam_init = 0.8 - 0.6 * math.exp(-0.3 * l)
        ya = _diff_attention(p3, diff_lambda[l], diff_subln_w[l], rel_bias, bias_a, lam_init, tq)
        yb = _swa(p3, swa_sinks[l], bias_b)
        yc = _ssd(p3, dt3, ssd_conv_w[l], ssd_conv_b[l], ssd_dt_bias[l], ssd_a_log[l],
                  ssd_d[l], ssd_norm_w[l])
        yd = _conformer(p3, conf_conv_w[l], conf_conv_b[l], conf_ln_w[l], conf_ln_b[l], 256)

        ys = [y.reshape(t, BRANCH_WIDTH) for y in (ya, yb, yc, yd)]
        merged = _merge(h2, ys, w_gate, w_br, 1024, 256)
        last = l == depth - 1
        next_w = final_norm_w if last else norm_w[l + 1]
        res = _outproj(merged, w_o, x2, next_w, last, 512)
        if last:
            out = res
        else:
            x2, h2 = res
    return out.reshape(b, s, d)
```

```python
import functools
import math

import jax
import jax.numpy as jnp
import numpy as np
from jax import lax
from jax.experimental import pallas as pl
from jax.experimental.pallas import tpu as pltpu

F32 = jnp.float32
BF16 = jnp.bfloat16

D_MODEL = 2048
NORM_EPS = 1e-6
BRANCH_WIDTH = 1024
NUM_BUCKETS = 32
MAX_DISTANCE = 128

A_QK = 64
A_V = 128
A_HEADS = 8
B_HD = 64
B_HEADS = 16
B_KV = 4
WINDOW = 128
C_HEADS = 16
C_HD = 64
C_GROUPS = 2
C_STATE = 128
C_CONV = 4
CHUNK = 128
C_XBC = 1536
D_CONV = 31
N_BRANCH = 4

COL_A = 0
COL_BQ = 4096
COL_BG = 5120
COL_CXBC = 6144
COL_BK = 7680
COL_BV = 7936
COL_CZ = 8192
COL_DG = 9216
COL_DGLU = 10240
N_MIX = 12288
_SRC_RANGES = (
    (0, 4096),
    (4096, 5120),
    (5632, 6656),
    (6656, 8192),
    (5120, 5376),
    (5376, 5632),
    (8208, 9232),
    (11280, 12304),
    (9232, 11280),
)
SRC_DT = (8192, 8208)
SRC_MG = 12304

NEG = -1e30
LOG2E = 1.4426950408889634
LANES = 128
VMEM_LIMIT = 56 * 1024 * 1024


def _cparams(*sem):
    return pltpu.CompilerParams(dimension_semantics=sem, vmem_limit_bytes=VMEM_LIMIT)


def _sigmoid(x):
    return 1.0 / (1.0 + jnp.exp(-x))


def _silu(x):
    return x * _sigmoid(x)


def _rmsnorm_kernel(x_ref, w_ref, o_ref):
    x = x_ref[...]
    ms = jnp.mean(x * x, axis=-1, keepdims=True)
    o_ref[...] = (x * lax.rsqrt(ms + NORM_EPS) * w_ref[...]).astype(o_ref.dtype)


def _rmsnorm(x2, w, out_dtype):
    t, d = x2.shape
    tm = min(512, t)
    return pl.pallas_call(
        _rmsnorm_kernel,
        grid=(t // tm,),
        in_specs=[pl.BlockSpec((tm, d), lambda i: (i, 0)),
                  pl.BlockSpec((1, d), lambda i: (0, 0))],
        out_specs=pl.BlockSpec((tm, d), lambda i: (i, 0)),
        out_shape=jax.ShapeDtypeStruct((t, d), out_dtype),
        compiler_params=_cparams("parallel"),
        name="rmsnorm",
    )(x2, w.reshape(1, d))


def _matmul_kernel(a_ref, b_ref, cs_ref, o_ref):
    acc = jnp.dot(a_ref[...], b_ref[...], preferred_element_type=F32)
    o_ref[...] = (acc * cs_ref[...]).astype(o_ref.dtype)


def _matmul(a, b, col_scale, out_dtype, tm, tn, name):
    m, k = a.shape
    n = b.shape[1]
    tm = min(tm, m)
    tn = min(tn, n)
    return pl.pallas_call(
        _matmul_kernel,
        grid=(m // tm, n // tn),
        in_specs=[pl.BlockSpec((tm, k), lambda i, j: (i, 0)),
                  pl.BlockSpec((k, tn), lambda i, j: (0, j)),
                  pl.BlockSpec((1, tn), lambda i, j: (0, j))],
        out_specs=pl.BlockSpec((tm, tn), lambda i, j: (i, j)),
        out_shape=jax.ShapeDtypeStruct((m, n), out_dtype),
        compiler_params=_cparams("parallel", "parallel"),
        name=name,
    )(a, b, col_scale)


def _bucket_lower_bounds():
    max_exact = NUM_BUCKETS // 2
    lower = list(range(max_exact))
    for k in range(max_exact, NUM_BUCKETS):
        d = max_exact
        while True:
            large = max_exact + int(math.log(d / max_exact) / math.log(MAX_DISTANCE / max_exact)
                                    * (NUM_BUCKETS - max_exact))
            if min(large, NUM_BUCKETS - 1) >= k:
                break
            d += 1
        lower.append(d)
    return tuple(lower)


BUCKET_LOWER = _bucket_lower_bounds()


def _bias_of_distance(d, tbl_ref, h):
    val = jnp.zeros(d.shape, F32) + tbl_ref[0, h] * LOG2E
    for k in range(1, NUM_BUCKETS):
        val = jnp.where(d >= BUCKET_LOWER[k], tbl_ref[k, h] * LOG2E, val)
    return val


def _bias_tiles_kernel(tbl_ref, o_ref, *, n, head0, window):
    h = pl.program_id(0) + head0
    row = lax.broadcasted_iota(jnp.int32, (n, n), 0)
    col = lax.broadcasted_iota(jnp.int32, (n, n), 1)
    d_prev = row - col + n
    d_cur = row - col
    prev = _bias_of_distance(d_prev, tbl_ref, h)
    if window is not None:
        prev = jnp.where(d_prev < window, prev, NEG)
    o_ref[:, :n] = prev
    o_ref[:, n:] = jnp.where(d_cur >= 0, _bias_of_distance(jnp.maximum(d_cur, 0), tbl_ref, h), NEG)


def _bias_tiles(rel_bias, n, head0, heads, window):
    return pl.pallas_call(
        functools.partial(_bias_tiles_kernel, n=n, head0=head0, window=window),
        grid=(heads,),
        in_specs=[pl.BlockSpec(memory_space=pltpu.SMEM)],
        out_specs=pl.BlockSpec((None, n, 2 * n), lambda h: (h, 0, 0)),
        out_shape=jax.ShapeDtypeStruct((heads, n, 2 * n), F32),
        compiler_params=_cparams("parallel"),
        name="bias_tiles",
    )(rel_bias)


def _lane_fold(x, op):
    r = x[:, :LANES]
    for c in range(1, x.shape[1] // LANES):
        r = op(r, x[:, c * LANES:(c + 1) * LANES])
    return r


def _diffattn_kernel(tbl_ref, lam_ref, q_ref, k_ref, v_ref, g_ref, bias_ref, subw_ref,
                     o_ref, s_ref, *, tq, nq, lam_init):
    h = pl.program_id(1)
    reps = tq // LANES
    dn_nt = (((1,), (1,)), ((), ()))
    cfar = tbl_ref[NUM_BUCKETS - 1, h] * LOG2E
    lp = lam_ref[...]
    lam = (jnp.exp(jnp.sum(lp[0:1] * lp[1:2], axis=1, keepdims=True))
           - jnp.exp(jnp.sum(lp[2:3] * lp[3:4], axis=1, keepdims=True)) + lam_init)
    lane = lax.broadcasted_iota(jnp.int32, (tq, LANES), 1)

    def with_bias(s, b):
        return jnp.concatenate([s[:tq] + b, s[tq:] + b], axis=0)

    for i in range(nq):
        rows = slice(i * tq, (i + 1) * tq)
        q = q_ref[rows, :]
        zero = jnp.zeros_like(q)
        qz = jnp.concatenate([jnp.where(lane < A_QK, q, zero),
                              jnp.where(lane >= A_QK, q, zero)], axis=0)
        slot = i % 2
        mp_far = None
        mp_near = None
        for j in range(i + 1):
            s = lax.dot_general(qz, k_ref[j * tq:(j + 1) * tq, :], dn_nt,
                                preferred_element_type=F32)
            if j == i:
                s = with_bias(s, bias_ref[:, tq:])
            elif j == i - 1:
                s = with_bias(s, bias_ref[:, :tq])
            s_ref[slot, j] = s
            f = _lane_fold(s, jnp.maximum)
            if j >= i - 1:
                mp_near = f if mp_near is None else jnp.maximum(mp_near, f)
            else:
                mp_far = f if mp_far is None else jnp.maximum(mp_far, f)
        mp = mp_near if mp_far is None else jnp.maximum(mp_near, mp_far + cfar)
        m = jnp.broadcast_to(jnp.max(mp, axis=1, keepdims=True), mp.shape)
        m_near = jnp.concatenate([m] * reps, axis=1)
        m_far = jnp.concatenate([m - cfar] * reps, axis=1)
        lpart = None
        acc = None
        for j in range(i + 1):
            p = jnp.exp2(s_ref[slot, j] - (m_near if j >= i - 1 else m_far))
            f = _lane_fold(p, jnp.add)
            pv = jnp.dot(p.astype(BF16), v_ref[j * tq:(j + 1) * tq, :],
                         preferred_element_type=F32)
            lpart = f if lpart is None else lpart + f
            acc = pv if acc is None else acc + pv
        lsum = jnp.broadcast_to(jnp.sum(lpart, axis=1, keepdims=True), lpart.shape)
        o_all = acc / lsum
        o = o_all[:tq] - lam * o_all[tq:]
        ms = jnp.mean(o * o, axis=1, keepdims=True)
        y = o * lax.rsqrt(ms + NORM_EPS) * subw_ref[...] * (1.0 - lam_init)
        o_ref[rows, :] = (y * _silu(g_ref[rows, :].astype(F32))).astype(o_ref.dtype)


def _diff_attention(p3, lam_params, subln_w, rel_bias, bias, lam_init, tq):
    b, s, _ = p3.shape
    nq = s // tq
    assert tq + 1 >= BUCKET_LOWER[-1]

    kernel = functools.partial(_diffattn_kernel, tq=tq, nq=nq, lam_init=lam_init)
    col = lambda off: (lambda bi, h: (bi, 0, off + h))
    return pl.pallas_call(
        kernel,
        grid=(b, A_HEADS),
        in_specs=[
            pl.BlockSpec(memory_space=pltpu.SMEM),
            pl.BlockSpec((4, A_QK), lambda bi, h: (0, 0)),
            pl.BlockSpec((None, s, LANES), col(0)),
            pl.BlockSpec((None, s, LANES), col(A_HEADS)),
            pl.BlockSpec((None, s, LANES), col(2 * A_HEADS)),
            pl.BlockSpec((None, s, LANES), col(3 * A_HEADS)),
            pl.BlockSpec((None, tq, 2 * tq), lambda bi, h: (h, 0, 0)),
            pl.BlockSpec((1, A_V), lambda bi, h: (0, 0)),
        ],
        out_specs=pl.BlockSpec((None, s, LANES), lambda bi, h: (bi, 0, h)),
        out_shape=jax.ShapeDtypeStruct((b, s, BRANCH_WIDTH), BF16),
        scratch_shapes=[pltpu.VMEM((2, nq, 2 * tq, tq), F32)],
        compiler_params=_cparams("parallel", "parallel"),
        name="diff_attention",
    )(rel_bias, lam_params, p3, p3, p3, p3, bias, subln_w.reshape(1, A_V))


def _swa_kernel(sink_ref, q_ref, g_ref, kc_ref, kp_ref, vc_ref, vp_ref, bias_ref, o_ref):
    n = pl.program_id(1)
    w = WINDOW
    dn_nt = (((1,), (1,)), ((), ()))
    group = B_HEADS // B_KV
    low = lax.broadcasted_iota(jnp.int32, (w, LANES), 1) < B_HD
    low2 = lax.broadcasted_iota(jnp.int32, (2 * w, LANES), 1) < B_HD
    key_col = lax.broadcasted_iota(jnp.int32, (1, 2 * w), 1)
    pen = jnp.where(key_col < w, jnp.where(n > 0, 0.0, NEG), 0.0).astype(F32)
    kcat = jnp.concatenate([kp_ref[...], kc_ref[...]], axis=0)
    vcat = jnp.concatenate([vp_ref[...], vc_ref[...]], axis=0)

    def dup_halves(x):
        xf = x.astype(F32)
        rolled = pltpu.roll(xf, B_HD, 1)
        return (jnp.where(low2, xf, rolled).astype(BF16),
                jnp.where(low2, rolled, xf).astype(BF16))

    for kvcol in range(B_KV * B_HD // LANES):
        kz = dup_halves(kcat[:, kvcol * LANES:(kvcol + 1) * LANES])
        vz = dup_halves(vcat[:, kvcol * LANES:(kvcol + 1) * LANES])
        for half in range(2):
            kv = 2 * kvcol + half
            qcols = [2 * kv + c for c in range(group // 2)]
            rows = []
            for c in qcols:
                qc = q_ref[:, c * LANES:(c + 1) * LANES]
                zero = jnp.zeros_like(qc)
                rows += [jnp.where(low, qc, zero), jnp.where(low, zero, qc)]
            qz = jnp.concatenate(rows, axis=0)
            s = lax.dot_general(qz, kz[half], dn_nt, preferred_element_type=F32)
            s = s + bias_ref[kv * group:(kv + 1) * group].reshape(group * w, 2 * w) + pen
            sink = jnp.concatenate(
                [jnp.zeros((w, LANES), F32) + sink_ref[kv * group + r] * LOG2E for r in range(group)],
                axis=0)
            m = jnp.max(_lane_fold(s, jnp.maximum), axis=1, keepdims=True)
            m = jnp.maximum(jnp.broadcast_to(m, sink.shape), sink)
            p = jnp.exp2(s - jnp.concatenate([m] * (2 * w // LANES), axis=1))
            l = jnp.sum(_lane_fold(p, jnp.add), axis=1, keepdims=True)
            l = jnp.broadcast_to(l, sink.shape) + jnp.exp2(sink - m)
            o = jnp.dot(p.astype(BF16), vz[half], preferred_element_type=F32) / l
            for ci, c in enumerate(qcols):
                pair = jnp.where(low, o[2 * ci * w:(2 * ci + 1) * w], o[(2 * ci + 1) * w:(2 * ci + 2) * w])
                cols = slice(c * LANES, (c + 1) * LANES)
                o_ref[:, cols] = (pair * _silu(g_ref[:, cols].astype(F32))).astype(o_ref.dtype)


def _swa(p3, sinks, bias):
    b, s, _ = p3.shape
    nb = s // WINDOW

    w = WINDOW
    prev_idx = lambda bi, n, c: (bi, jnp.maximum(n - 1, 0), c)
    return pl.pallas_call(
        _swa_kernel,
        grid=(b, nb),
        in_specs=[
            pl.BlockSpec(memory_space=pltpu.SMEM),
            pl.BlockSpec((None, w, 1024), lambda bi, n: (bi, n, COL_BQ // 1024)),
            pl.BlockSpec((None, w, 1024), lambda bi, n: (bi, n, COL_BG // 1024)),
            pl.BlockSpec((None, w, 256), lambda bi, n: (bi, n, COL_BK // 256)),
            pl.BlockSpec((None, w, 256), lambda bi, n: prev_idx(bi, n, COL_BK // 256)),
            pl.BlockSpec((None, w, 256), lambda bi, n: (bi, n, COL_BV // 256)),
            pl.BlockSpec((None, w, 256), lambda bi, n: prev_idx(bi, n, COL_BV // 256)),
            pl.BlockSpec((B_HEADS, w, 2 * w), lambda bi, n: (0, 0, 0)),
        ],
        out_specs=pl.BlockSpec((None, w, 1024), lambda bi, n: (bi, n, 0)),
        out_shape=jax.ShapeDtypeStruct((b, s, BRANCH_WIDTH), BF16),
        compiler_params=_cparams("parallel", "parallel"),
        name="swa",
    )(sinks, p3, p3, p3, p3, p3, p3, bias)


def _exact_dot01(v, r):
    hi = v.astype(BF16)
    r1 = v - hi.astype(F32)
    mid = r1.astype(BF16)
    lo = (r1 - mid.astype(F32)).astype(BF16)
    return (jnp.dot(hi, r, preferred_element_type=F32)
            + jnp.dot(mid, r, preferred_element_type=F32)
            + jnp.dot(lo, r, preferred_element_type=F32))


def _ssd_kernel(xbc_ref, z_ref, dt_ref, cw_ref, cb_ref, dtb_ref, alog_ref, dexp_ref, nw_ref,
                r64_ref, r128_ref, o_ref, ext_ref, state_ref):
    c = pl.program_id(1)
    halo = 8

    @pl.when(c == 0)
    def _():
        ext_ref[0:halo, :] = jnp.zeros((halo, C_XBC), F32)
        state_ref[...] = jnp.zeros(state_ref.shape, F32)

    xraw = xbc_ref[...].astype(F32)
    ext_ref[halo:halo + CHUNK, :] = xraw
    acc = jnp.broadcast_to(cb_ref[...], (CHUNK, C_XBC))
    for j in range(C_CONV):
        off = halo - (C_CONV - 1) + j
        acc = acc + cw_ref[j:j + 1, :] * ext_ref[off:off + CHUNK, :]
    ext_ref[0:halo, :] = xraw[CHUNK - halo:, :]
    xc = _silu(acc)
    xs = xc[:, :1024]
    bm = xc[:, 1024:1024 + C_GROUPS * C_STATE]
    cm = xc[:, 1024 + C_GROUPS * C_STATE:]

    dtx = dt_ref[...] + dtb_ref[...]
    dt = jnp.maximum(dtx, 0.0) + jnp.log1p(jnp.exp(-jnp.abs(dtx)))
    da = dt * (-jnp.exp(alog_ref[...]))
    row = lax.broadcasted_iota(jnp.int32, (CHUNK, CHUNK), 0)
    col = lax.broadcasted_iota(jnp.int32, (CHUNK, CHUNK), 1)
    causal = row >= col
    tri = causal.astype(BF16)
    hi = da.astype(BF16)
    r1 = da - hi.astype(F32)
    mid = r1.astype(BF16)
    lo = (r1 - mid.astype(F32)).astype(BF16)
    a_cum = (jnp.dot(tri, hi, preferred_element_type=F32)
             + jnp.dot(tri, mid, preferred_element_type=F32)
             + jnp.dot(tri, lo, preferred_element_type=F32))
    a_cum_t = a_cum.T

    r64 = r64_ref[...]
    dt_e = _exact_dot01(dt, r64)
    acum_e = _exact_dot01(a_cum, r64)
    acum_col = _exact_dot01(a_cum, r128_ref[...])
    alast_e = acum_e[CHUNK - 1:CHUNK, :]

    xdt = xs * dt_e
    xdt_b = xdt.astype(BF16)
    xw_b = (xdt * jnp.exp(alast_e - acum_e)).astype(BF16)
    chunk_decay = jnp.exp(alast_e)
    bm_b = bm.astype(BF16)
    cm_b = cm.astype(BF16)
    dn_nt = (((1,), (1,)), ((), ()))
    heads_per_group = C_HEADS // C_GROUPS
    gw = heads_per_group * C_HD

    y_parts = []
    for g in range(C_GROUPS):
        bg = bm[:, g * C_STATE:(g + 1) * C_STATE]
        bg_b = bm_b[:, g * C_STATE:(g + 1) * C_STATE]
        cg_b = cm_b[:, g * C_STATE:(g + 1) * C_STATE]
        cb = lax.dot_general(cg_b, bg_b, dn_nt, preferred_element_type=F32)
        bgt_b = bg.T.astype(BF16)
        gcols = slice(g * gw, (g + 1) * gw)
        new_states = jnp.dot(bgt_b, xw_b[:, gcols], preferred_element_type=F32)
        prev = state_ref[g]
        y_off = jnp.dot(cg_b, prev.astype(BF16), preferred_element_type=F32)
        state_ref[g] = prev * chunk_decay[:, gcols] + new_states
        diag = []
        for e in range(heads_per_group):
            hh = g * heads_per_group + e
            seg = acum_col[:, hh * LANES:(hh + 1) * LANES] - a_cum_t[hh:hh + 1, :]
            decay = jnp.exp(jnp.where(causal, seg, NEG))
            mh = (cb * decay).astype(BF16)
            diag.append(jnp.dot(mh, xdt_b[:, hh * C_HD:(hh + 1) * C_HD],
                                preferred_element_type=F32))
        y_diag = jnp.concatenate(diag, axis=1)
        y_g = (y_diag + y_off * jnp.exp(acum_e[:, gcols])
               + xs[:, gcols] * dexp_ref[:, gcols])
        zg = z_ref[:, gcols].astype(F32)
        yg = y_g * _silu(zg)
        ms = jnp.mean(yg * yg, axis=1, keepdims=True)
        y_parts.append(yg * lax.rsqrt(ms + NORM_EPS) * nw_ref[:, gcols])
    o_ref[...] = jnp.concatenate(y_parts, axis=1).astype(o_ref.dtype)


def _ssd(p3, dt3, conv_w, conv_b, dt_bias, a_log, d_skip, norm_w):
    b, s, _ = p3.shape
    nc = s // CHUNK
    pad = LANES - C_HEADS
    dtb = jnp.pad(dt_bias, (0, pad)).reshape(1, LANES)
    alog = jnp.pad(a_log, (0, pad)).reshape(1, LANES)
    dexp = jnp.repeat(d_skip, C_HD).reshape(1, BRANCH_WIDTH)
    heads = np.arange(LANES)[:, None]
    r64 = jnp.asarray(heads == (np.arange(C_HEADS * C_HD)[None, :] // C_HD), BF16)
    r128 = jnp.asarray(heads == (np.arange(C_HEADS * LANES)[None, :] // LANES), BF16)
    const = lambda bi, c: (0, 0)
    return pl.pallas_call(
        _ssd_kernel,
        grid=(b, nc),
        in_specs=[
            pl.BlockSpec((None, CHUNK, C_XBC), lambda bi, c: (bi, c, COL_CXBC // C_XBC)),
            pl.BlockSpec((None, CHUNK, 1024), lambda bi, c: (bi, c, COL_CZ // 1024)),
            pl.BlockSpec((None, CHUNK, LANES), lambda bi, c: (bi, c, 0)),
            pl.BlockSpec((C_CONV, C_XBC), const),
            pl.BlockSpec((1, C_XBC), const),
            pl.BlockSpec((1, LANES), const),
            pl.BlockSpec((1, LANES), const),
            pl.BlockSpec((1, BRANCH_WIDTH), const),
            pl.BlockSpec((1, BRANCH_WIDTH), const),
            pl.BlockSpec((LANES, C_HEADS * C_HD), const),
            pl.BlockSpec((LANES, C_HEADS * LANES), const),
        ],
        out_specs=pl.BlockSpec((None, CHUNK, BRANCH_WIDTH), lambda bi, c: (bi, c, 0)),
        out_shape=jax.ShapeDtypeStruct((b, s, BRANCH_WIDTH), BF16),
        scratch_shapes=[pltpu.VMEM((CHUNK + 8, C_XBC), F32),
                        pltpu.VMEM((C_GROUPS, C_STATE, 512), F32)],
        compiler_params=_cparams("parallel", "arbitrary"),
        name="ssd",
    )(p3, p3, dt3, conv_w, conv_b.reshape(1, C_XBC), dtb, alog, dexp,
      norm_w.reshape(1, BRANCH_WIDTH), r64, r128)


CONF_HALO = 32


def _conformer_kernel(glu_ref, g_ref, cw_ref, cb_ref, lnw_ref, lnb_ref, o_ref, ext_ref, *, ts):
    t = pl.program_id(1)

    @pl.when(t == 0)
    def _():
        ext_ref[0:CONF_HALO, :] = jnp.zeros((CONF_HALO, BRANCH_WIDTH), F32)

    glu = glu_ref[...].astype(F32)
    hcur = glu[:, :BRANCH_WIDTH] * _sigmoid(glu[:, BRANCH_WIDTH:])
    ext_ref[CONF_HALO:CONF_HALO + ts, :] = hcur
    acc = jnp.broadcast_to(cb_ref[...], (ts, BRANCH_WIDTH))
    for j in range(D_CONV):
        off = CONF_HALO - (D_CONV - 1) + j
        acc = acc + cw_ref[j:j + 1, :] * ext_ref[off:off + ts, :]
    ext_ref[0:CONF_HALO, :] = hcur[ts - CONF_HALO:, :]
    mu = jnp.mean(acc, axis=1, keepdims=True)
    xc = acc - mu
    var = jnp.mean(xc * xc, axis=1, keepdims=True)
    y = xc * lax.rsqrt(var + NORM_EPS) * lnw_ref[...] + lnb_ref[...]
    o_ref[...] = (_silu(y) * _silu(g_ref[...].astype(F32))).astype(o_ref.dtype)


def _conformer(p3, conv_w, conv_b, ln_w, ln_b, ts):
    b, s, _ = p3.shape
    ts = min(ts, s)
    const = lambda bi, t: (0, 0)
    w = BRANCH_WIDTH
    return pl.pallas_call(
        functools.partial(_conformer_kernel, ts=ts),
        grid=(b, s // ts),
        in_specs=[
            pl.BlockSpec((None, ts, 2 * w), lambda bi, t: (bi, t, COL_DGLU // (2 * w))),
            pl.BlockSpec((None, ts, w), lambda bi, t: (bi, t, COL_DG // w)),
            pl.BlockSpec((D_CONV, w), const),
            pl.BlockSpec((1, w), const),
            pl.BlockSpec((1, w), const),
            pl.BlockSpec((1, w), const),
        ],
        out_specs=pl.BlockSpec((None, ts, w), lambda bi, t: (bi, t, 0)),
        out_shape=jax.ShapeDtypeStruct((b, s, w), BF16),
        scratch_shapes=[pltpu.VMEM((CONF_HALO + ts, w), F32)],
        compiler_params=_cparams("parallel", "arbitrary"),
        name="conformer",
    )(p3, p3, conv_w, conv_b.reshape(1, w), ln_w.reshape(1, w), ln_b.reshape(1, w))


def _merge_kernel(h_ref, ya_ref, yb_ref, yc_ref, yd_ref, wg0, wg1, wg2, wg3, wb_ref, o_ref):
    h = h_ref[...]
    acc = None
    for i, (y_ref, wg_ref) in enumerate(((ya_ref, wg0), (yb_ref, wg1), (yc_ref, wg2), (yd_ref, wg3))):
        gate = jnp.dot(h, wg_ref[...], preferred_element_type=F32)
        up = jnp.dot(y_ref[...], wb_ref[i], preferred_element_type=F32)
        term = _sigmoid(gate) * up
        acc = term if acc is None else acc + term
    o_ref[...] = acc.astype(o_ref.dtype)


def _merge(h2, ys, wg, wb, tm, tn):
    t, d = h2.shape
    tm = min(tm, t)
    nblk = d // tn
    y_spec = pl.BlockSpec((tm, BRANCH_WIDTH), lambda i, j: (i, 0))
    wg_specs = [pl.BlockSpec((d, tn), functools.partial(lambda i, j, k: (0, k * nblk + j), k=k))
                for k in range(N_BRANCH)]
    return pl.pallas_call(
        _merge_kernel,
        grid=(t // tm, nblk),
        in_specs=[pl.BlockSpec((tm, d), lambda i, j: (i, 0)), y_spec, y_spec, y_spec, y_spec,
                  *wg_specs,
                  pl.BlockSpec((N_BRANCH, BRANCH_WIDTH, tn), lambda i, j: (0, 0, j))],
        out_specs=pl.BlockSpec((tm, tn), lambda i, j: (i, j)),
        out_shape=jax.ShapeDtypeStruct((t, d), BF16),
        compiler_params=_cparams("parallel", "parallel"),
        name="merge",
    )(h2, *ys, wg, wg, wg, wg, wb)


def _outproj_kernel(m_ref, w_ref, x_ref, nw_ref, *out_refs, last):
    xn = x_ref[...] + jnp.dot(m_ref[...], w_ref[...], preferred_element_type=F32)
    ms = jnp.mean(xn * xn, axis=-1, keepdims=True)
    normed = xn * lax.rsqrt(ms + NORM_EPS) * nw_ref[...]
    if last:
        out_refs[0][...] = normed
    else:
        out_refs[0][...] = xn
        out_refs[1][...] = normed.astype(BF16)


def _outproj(merged, w_out, x2, next_norm_w, last, tm):
    t, d = x2.shape
    tm = min(tm, t)
    row = pl.BlockSpec((tm, d), lambda i: (i, 0))
    if last:
        out_specs = row
        out_shape = jax.ShapeDtypeStruct((t, d), F32)
    else:
        out_specs = (row, row)
        out_shape = (jax.ShapeDtypeStruct((t, d), F32), jax.ShapeDtypeStruct((t, d), BF16))
    return pl.pallas_call(
        functools.partial(_outproj_kernel, last=last),
        grid=(t // tm,),
        in_specs=[row, pl.BlockSpec((d, d), lambda i: (0, 0)), row,
                  pl.BlockSpec((1, d), lambda i: (0, 0))],
        out_specs=out_specs,
        out_shape=out_shape,
        compiler_params=_cparams("parallel"),
        name="outproj",
    )(merged, w_out, x2, next_norm_w.reshape(1, d))


def kernel(x, norm_w, w_in, diff_lambda, diff_subln_w, swa_sinks, ssd_conv_w, ssd_conv_b,
           ssd_dt_bias, ssd_a_log, ssd_d, ssd_norm_w, conf_conv_w, conf_conv_b, conf_ln_w,
           conf_ln_b, w_branch, w_out, rel_bias, final_norm_w):
    b, s, d = x.shape
    depth = w_in.shape[0]
    t = b * s
    tq = min(256, s)
    bias_a = _bias_tiles(rel_bias, tq, 0, A_HEADS, None)
    bias_b = _bias_tiles(rel_bias, WINDOW, A_HEADS, B_HEADS, WINDOW)

    mix_scale = np.ones((1, N_MIX), np.float32)
    mix_scale[0, COL_A:COL_A + A_HEADS * 2 * A_QK] = A_QK ** -0.5 * LOG2E
    mix_scale[0, COL_BQ:COL_BQ + B_HEADS * B_HD] = B_HD ** -0.5 * LOG2E
    mix_scale = jnp.asarray(mix_scale)

    x2 = x.reshape(t, d)
    h2 = _rmsnorm(x2, norm_w[0], BF16)
    out = None
    for l in range(depth):
        wl = w_in[l]
        w_mix = jnp.concatenate([wl[:, a:z] for a, z in _SRC_RANGES], axis=1).astype(BF16)
        w_dt = jnp.pad(wl[:, SRC_DT[0]:SRC_DT[1]], ((0, 0), (0, LANES - C_HEADS))).astype(BF16)
        w_gate = wl[:, SRC_MG:].astype(BF16)
        w_br = w_branch[l].astype(BF16)
        w_o = w_out[l].astype(BF16)

        proj = _matmul(h2, w_mix, mix_scale, BF16, 1024, 1024, "inproj")
        dt_raw = _matmul(h2, w_dt, jnp.ones((1, LANES), F32), F32, 1024, LANES, "dtproj")
        p3 = proj.reshape(b, s, N_MIX)
        dt3 = dt_raw.reshape(b, s, LANES)

        lam_init = 0.8 - 0.6 * math.exp(-0.3 * l)
        ya = _diff_attention(p3, diff_lambda[l], diff_subln_w[l], rel_bias, bias_a, lam_init, tq)
        yb = _swa(p3, swa_sinks[l], bias_b)
        yc = _ssd(p3, dt3, ssd_conv_w[l], ssd_conv_b[l], ssd_dt_bias[l], ssd_a_log[l],
                  ssd_d[l], ssd_norm_w[l])
        yd = _conformer(p3, conf_conv_w[l], conf_conv_b[l], conf_ln_w[l], conf_ln_b[l], 256)

        ys = [y.reshape(t, BRANCH_WIDTH) for y in (ya, yb, yc, yd)]
        merged = _merge(h2, ys, w_gate, w_br, 1024, 256)
        last = l == depth - 1
        next_w = final_norm_w if last else norm_w[l + 1]
        res = _outproj(merged, w_o, x2, next_w, last, 512)
        if last:
            out = res
        else:
            x2, h2 = res
    return out.reshape(b, s, d)
```

```python
import functools
import math

import jax
import jax.numpy as jnp
import numpy as np
from jax import lax
from jax.experimental import pallas as pl
from jax.experimental.pallas import tpu as pltpu

F32 = jnp.float32
BF16 = jnp.bfloat16

D_MODEL = 2048
NORM_EPS = 1e-6
BRANCH_WIDTH = 1024
NUM_BUCKETS = 32
MAX_DISTANCE = 128

A_QK = 64
A_V = 128
A_HEADS = 8
B_HD = 64
B_HEADS = 16
B_KV = 4
WINDOW = 128
C_HEADS = 16
C_HD = 64
C_GROUPS = 2
C_STATE = 128
C_CONV = 4
CHUNK = 128
C_XBC = 1536
D_CONV = 31
N_BRANCH = 4

N_FIRST = 8192
COL_BQ = 4096
COL_BK = 5120
COL_BV = 5376
COL_BG = 5632
COL_CXBC = 6656
SRC_DT = (8192, 8208)
SRC_SECOND = (8208, 12304)
COL2_CZ = 0
COL2_DGLU = 1024
COL2_DG = 3072
SRC_MG = 12304

NEG = -1e30
LOG2E = 1.4426950408889634
LANES = 128
VMEM_LIMIT = 56 * 1024 * 1024


def _cparams(*sem):
    return pltpu.CompilerParams(dimension_semantics=sem, vmem_limit_bytes=VMEM_LIMIT)


def _sigmoid(x):
    return 1.0 / (1.0 + jnp.exp(-x))


def _silu(x):
    return x * _sigmoid(x)


def _rmsnorm_kernel(x_ref, w_ref, o_ref):
    x = x_ref[...]
    ms = jnp.mean(x * x, axis=-1, keepdims=True)
    o_ref[...] = (x * lax.rsqrt(ms + NORM_EPS) * w_ref[...]).astype(o_ref.dtype)


def _rmsnorm(x2, w, out_dtype):
    t, d = x2.shape
    tm = min(512, t)
    return pl.pallas_call(
        _rmsnorm_kernel,
        grid=(t // tm,),
        in_specs=[pl.BlockSpec((tm, d), lambda i: (i, 0)),
                  pl.BlockSpec((1, d), lambda i: (0, 0))],
        out_specs=pl.BlockSpec((tm, d), lambda i: (i, 0)),
        out_shape=jax.ShapeDtypeStruct((t, d), out_dtype),
        compiler_params=_cparams("parallel"),
        name="rmsnorm",
    )(x2, w.reshape(1, d))


def _matmul_kernel(a_ref, b_ref, cs_ref, o_ref):
    acc = jnp.dot(a_ref[...], b_ref[...], preferred_element_type=F32)
    o_ref[...] = (acc * cs_ref[...]).astype(o_ref.dtype)


def _matmul(a, b, layer, col_scale, out_dtype, tm, tn, name):
    m, k = a.shape
    n = b.shape[2]
    tm = min(tm, m)
    tn = min(tn, n)
    return pl.pallas_call(
        _matmul_kernel,
        grid=(m // tm, n // tn),
        in_specs=[pl.BlockSpec((tm, k), lambda i, j: (i, 0)),
                  pl.BlockSpec((None, k, tn), lambda i, j: (layer, 0, j)),
                  pl.BlockSpec((1, tn), lambda i, j: (0, j))],
        out_specs=pl.BlockSpec((tm, tn), lambda i, j: (i, j)),
        out_shape=jax.ShapeDtypeStruct((m, n), out_dtype),
        compiler_params=_cparams("parallel", "parallel"),
        name=name,
    )(a, b, col_scale)


def _bucket_lower_bounds():
    max_exact = NUM_BUCKETS // 2
    lower = list(range(max_exact))
    for k in range(max_exact, NUM_BUCKETS):
        d = max_exact
        while True:
            large = max_exact + int(math.log(d / max_exact) / math.log(MAX_DISTANCE / max_exact)
                                    * (NUM_BUCKETS - max_exact))
            if min(large, NUM_BUCKETS - 1) >= k:
                break
            d += 1
        lower.append(d)
    return tuple(lower)


BUCKET_LOWER = _bucket_lower_bounds()


def _bias_of_distance(d, tbl_ref, h):
    val = jnp.zeros(d.shape, F32) + tbl_ref[0, h] * LOG2E
    for k in range(1, NUM_BUCKETS):
        val = jnp.where(d >= BUCKET_LOWER[k], tbl_ref[k, h] * LOG2E, val)
    return val


def _bias_tiles_kernel(tbl_ref, o_ref, *, n, head0, window):
    h = pl.program_id(0) + head0
    row = lax.broadcasted_iota(jnp.int32, (n, n), 0)
    col = lax.broadcasted_iota(jnp.int32, (n, n), 1)
    d_prev = row - col + n
    d_cur = row - col
    prev = _bias_of_distance(d_prev, tbl_ref, h)
    if window is not None:
        prev = jnp.where(d_prev < window, prev, NEG)
    o_ref[:, :n] = prev
    o_ref[:, n:] = jnp.where(d_cur >= 0, _bias_of_distance(jnp.maximum(d_cur, 0), tbl_ref, h), NEG)


def _bias_tiles(rel_bias, n, head0, heads, window):
    return pl.pallas_call(
        functools.partial(_bias_tiles_kernel, n=n, head0=head0, window=window),
        grid=(heads,),
        in_specs=[pl.BlockSpec(memory_space=pltpu.SMEM)],
        out_specs=pl.BlockSpec((None, n, 2 * n), lambda h: (h, 0, 0)),
        out_shape=jax.ShapeDtypeStruct((heads, n, 2 * n), F32),
        compiler_params=_cparams("parallel"),
        name="bias_tiles",
    )(rel_bias)


def _lane_fold(x, op):
    r = x[:, :LANES]
    for c in range(1, x.shape[1] // LANES):
        r = op(r, x[:, c * LANES:(c + 1) * LANES])
    return r


def _diffattn_kernel(tbl_ref, lam_ref, q_ref, k_ref, v_ref, g_ref, bias_ref, subw_ref,
                     o_ref, s_ref, *, tq, nq, lam_init):
    h = pl.program_id(1)
    reps = tq // LANES
    dn_nt = (((1,), (1,)), ((), ()))
    cfar = tbl_ref[NUM_BUCKETS - 1, h] * LOG2E
    lp = lam_ref[...]
    lam = (jnp.exp(jnp.sum(lp[0:1] * lp[1:2], axis=1, keepdims=True))
           - jnp.exp(jnp.sum(lp[2:3] * lp[3:4], axis=1, keepdims=True)) + lam_init)
    lane = lax.broadcasted_iota(jnp.int32, (tq, LANES), 1)

    def with_bias(s, b):
        return jnp.concatenate([s[:tq] + b, s[tq:] + b], axis=0)

    for i in range(nq):
        rows = slice(i * tq, (i + 1) * tq)
        q = q_ref[rows, :]
        zero = jnp.zeros_like(q)
        qz = jnp.concatenate([jnp.where(lane < A_QK, q, zero),
                              jnp.where(lane >= A_QK, q, zero)], axis=0)
        slot = i % 2
        mp_far = None
        mp_near = None
        for j in range(i + 1):
            s = lax.dot_general(qz, k_ref[j * tq:(j + 1) * tq, :], dn_nt,
                                preferred_element_type=F32)
            if j == i:
                s = with_bias(s, bias_ref[:, tq:])
            elif j == i - 1:
                s = with_bias(s, bias_ref[:, :tq])
            s_ref[slot, j] = s
            f = _lane_fold(s, jnp.maximum)
            if j >= i - 1:
                mp_near = f if mp_near is None else jnp.maximum(mp_near, f)
            else:
                mp_far = f if mp_far is None else jnp.maximum(mp_far, f)
        mp = mp_near if mp_far is None else jnp.maximum(mp_near, mp_far + cfar)
        m = jnp.broadcast_to(jnp.max(mp, axis=1, keepdims=True), mp.shape)
        m_near = jnp.concatenate([m] * reps, axis=1)
        m_far = jnp.concatenate([m - cfar] * reps, axis=1)
        lpart = None
        acc = None
        for j in range(i + 1):
            p = jnp.exp2(s_ref[slot, j] - (m_near if j >= i - 1 else m_far))
            f = _lane_fold(p, jnp.add)
            pv = jnp.dot(p.astype(BF16), v_ref[j * tq:(j + 1) * tq, :],
                         preferred_element_type=F32)
            lpart = f if lpart is None else lpart + f
            acc = pv if acc is None else acc + pv
        lsum = jnp.broadcast_to(jnp.sum(lpart, axis=1, keepdims=True), lpart.shape)
        o_all = acc / lsum
        o = o_all[:tq] - lam * o_all[tq:]
        ms = jnp.mean(o * o, axis=1, keepdims=True)
        y = o * lax.rsqrt(ms + NORM_EPS) * subw_ref[...] * (1.0 - lam_init)
        o_ref[rows, :] = (y * _silu(g_ref[rows, :].astype(F32))).astype(o_ref.dtype)


def _diff_attention(p3, lam_params, subln_w, rel_bias, bias, lam_init, tq):
    b, s, _ = p3.shape
    nq = s // tq
    assert tq + 1 >= BUCKET_LOWER[-1]

    kernel = functools.partial(_diffattn_kernel, tq=tq, nq=nq, lam_init=lam_init)
    col = lambda off: (lambda bi, h: (bi, 0, off + h))
    return pl.pallas_call(
        kernel,
        grid=(b, A_HEADS),
        in_specs=[
            pl.BlockSpec(memory_space=pltpu.SMEM),
            pl.BlockSpec((4, A_QK), lambda bi, h: (0, 0)),
            pl.BlockSpec((None, s, LANES), col(0)),
            pl.BlockSpec((None, s, LANES), col(A_HEADS)),
            pl.BlockSpec((None, s, LANES), col(2 * A_HEADS)),
            pl.BlockSpec((None, s, LANES), col(3 * A_HEADS)),
            pl.BlockSpec((None, tq, 2 * tq), lambda bi, h: (h, 0, 0)),
            pl.BlockSpec((1, A_V), lambda bi, h: (0, 0)),
        ],
        out_specs=pl.BlockSpec((None, s, LANES), lambda bi, h: (bi, 0, h)),
        out_shape=jax.ShapeDtypeStruct((b, s, BRANCH_WIDTH), BF16),
        scratch_shapes=[pltpu.VMEM((2, nq, 2 * tq, tq), F32)],
        compiler_params=_cparams("parallel", "parallel"),
        name="diff_attention",
    )(rel_bias, lam_params, p3, p3, p3, p3, bias, subln_w.reshape(1, A_V))


def _swa_kernel(sink_ref, q_ref, glo_ref, ghi_ref, kc_ref, kp_ref, vc_ref, vp_ref, bias_ref, o_ref):
    n = pl.program_id(1)
    w = WINDOW
    dn_nt = (((1,), (1,)), ((), ()))
    group = B_HEADS // B_KV
    low = lax.broadcasted_iota(jnp.int32, (w, LANES), 1) < B_HD
    low2 = lax.broadcasted_iota(jnp.int32, (2 * w, LANES), 1) < B_HD
    key_col = lax.broadcasted_iota(jnp.int32, (1, 2 * w), 1)
    pen = jnp.where(key_col < w, jnp.where(n > 0, 0.0, NEG), 0.0).astype(F32)
    kcat = jnp.concatenate([kp_ref[...], kc_ref[...]], axis=0)
    vcat = jnp.concatenate([vp_ref[...], vc_ref[...]], axis=0)

    def dup_halves(x):
        xf = x.astype(F32)
        rolled = pltpu.roll(xf, B_HD, 1)
        return (jnp.where(low2, xf, rolled).astype(BF16),
                jnp.where(low2, rolled, xf).astype(BF16))

    for kvcol in range(B_KV * B_HD // LANES):
        kz = dup_halves(kcat[:, kvcol * LANES:(kvcol + 1) * LANES])
        vz = dup_halves(vcat[:, kvcol * LANES:(kvcol + 1) * LANES])
        for half in range(2):
            kv = 2 * kvcol + half
            qcols = [2 * kv + c for c in range(group // 2)]
            rows = []
            for c in qcols:
                qc = q_ref[:, c * LANES:(c + 1) * LANES]
                zero = jnp.zeros_like(qc)
                rows += [jnp.where(low, qc, zero), jnp.where(low, zero, qc)]
            qz = jnp.concatenate(rows, axis=0)
            s = lax.dot_general(qz, kz[half], dn_nt, preferred_element_type=F32)
            s = s + bias_ref[kv * group:(kv + 1) * group].reshape(group * w, 2 * w) + pen
            sink = jnp.concatenate(
                [jnp.zeros((w, LANES), F32) + sink_ref[kv * group + r] * LOG2E for r in range(group)],
                axis=0)
            m = jnp.max(_lane_fold(s, jnp.maximum), axis=1, keepdims=True)
            m = jnp.maximum(jnp.broadcast_to(m, sink.shape), sink)
            p = jnp.exp2(s - jnp.concatenate([m] * (2 * w // LANES), axis=1))
            l = jnp.sum(_lane_fold(p, jnp.add), axis=1, keepdims=True)
            l = jnp.broadcast_to(l, sink.shape) + jnp.exp2(sink - m)
            o = jnp.dot(p.astype(BF16), vz[half], preferred_element_type=F32) / l
            for ci, c in enumerate(qcols):
                pair = jnp.where(low, o[2 * ci * w:(2 * ci + 1) * w], o[(2 * ci + 1) * w:(2 * ci + 2) * w])
                cols = slice(c * LANES, (c + 1) * LANES)
                half_cols = BRANCH_WIDTH // 2 // LANES
                g_ref = glo_ref if c < half_cols else ghi_ref
                gcols = slice((c % half_cols) * LANES, (c % half_cols + 1) * LANES)
                o_ref[:, cols] = (pair * _silu(g_ref[:, gcols].astype(F32))).astype(o_ref.dtype)


def _swa(p3, sinks, bias):
    b, s, _ = p3.shape
    nb = s // WINDOW

    w = WINDOW
    prev_idx = lambda bi, n, c: (bi, jnp.maximum(n - 1, 0), c)
    return pl.pallas_call(
        _swa_kernel,
        grid=(b, nb),
        in_specs=[
            pl.BlockSpec(memory_space=pltpu.SMEM),
            pl.BlockSpec((None, w, 1024), lambda bi, n: (bi, n, COL_BQ // 1024)),
            pl.BlockSpec((None, w, 512), lambda bi, n: (bi, n, COL_BG // 512)),
            pl.BlockSpec((None, w, 512), lambda bi, n: (bi, n, COL_BG // 512 + 1)),
            pl.BlockSpec((None, w, 256), lambda bi, n: (bi, n, COL_BK // 256)),
            pl.BlockSpec((None, w, 256), lambda bi, n: prev_idx(bi, n, COL_BK // 256)),
            pl.BlockSpec((None, w, 256), lambda bi, n: (bi, n, COL_BV // 256)),
            pl.BlockSpec((None, w, 256), lambda bi, n: prev_idx(bi, n, COL_BV // 256)),
            pl.BlockSpec((B_HEADS, w, 2 * w), lambda bi, n: (0, 0, 0)),
        ],
        out_specs=pl.BlockSpec((None, w, 1024), lambda bi, n: (bi, n, 0)),
        out_shape=jax.ShapeDtypeStruct((b, s, BRANCH_WIDTH), BF16),
        compiler_params=_cparams("parallel", "parallel"),
        name="swa",
    )(sinks, p3, p3, p3, p3, p3, p3, p3, bias)


def _exact_dot01(v, r):
    hi = v.astype(BF16)
    r1 = v - hi.astype(F32)
    mid = r1.astype(BF16)
    lo = (r1 - mid.astype(F32)).astype(BF16)
    return (jnp.dot(hi, r, preferred_element_type=F32)
            + jnp.dot(mid, r, preferred_element_type=F32)
            + jnp.dot(lo, r, preferred_element_type=F32))


def _ssd_kernel(xbc0_ref, xbc1_ref, xbc2_ref, z_ref, dt_ref, cw_ref, cb_ref, dtb_ref, alog_ref,
                dexp_ref, nw_ref, r64_ref, r128_ref, o_ref, ext_ref, state_ref):
    c = pl.program_id(1)
    halo = 8

    @pl.when(c == 0)
    def _():
        ext_ref[0:halo, :] = jnp.zeros((halo, C_XBC), F32)
        state_ref[...] = jnp.zeros(state_ref.shape, F32)

    xraw = jnp.concatenate([xbc0_ref[...], xbc1_ref[...], xbc2_ref[...]], axis=1).astype(F32)
    ext_ref[halo:halo + CHUNK, :] = xraw
    acc = jnp.broadcast_to(cb_ref[...], (CHUNK, C_XBC))
    for j in range(C_CONV):
        off = halo - (C_CONV - 1) + j
        acc = acc + cw_ref[j:j + 1, :] * ext_ref[off:off + CHUNK, :]
    ext_ref[0:halo, :] = xraw[CHUNK - halo:, :]
    xc = _silu(acc)
    xs = xc[:, :1024]
    bm = xc[:, 1024:1024 + C_GROUPS * C_STATE]
    cm = xc[:, 1024 + C_GROUPS * C_STATE:]

    dtx = dt_ref[...] + dtb_ref[...]
    dt = jnp.maximum(dtx, 0.0) + jnp.log1p(jnp.exp(-jnp.abs(dtx)))
    da = dt * (-jnp.exp(alog_ref[...]))
    row = lax.broadcasted_iota(jnp.int32, (CHUNK, CHUNK), 0)
    col = lax.broadcasted_iota(jnp.int32, (CHUNK, CHUNK), 1)
    causal = row >= col
    tri = causal.astype(BF16)
    hi = da.astype(BF16)
    r1 = da - hi.astype(F32)
    mid = r1.astype(BF16)
    lo = (r1 - mid.astype(F32)).astype(BF16)
    a_cum = (jnp.dot(tri, hi, preferred_element_type=F32)
             + jnp.dot(tri, mid, preferred_element_type=F32)
             + jnp.dot(tri, lo, preferred_element_type=F32))
    a_cum_t = a_cum.T

    r64 = r64_ref[...]
    dt_e = _exact_dot01(dt, r64)
    acum_e = _exact_dot01(a_cum, r64)
    acum_col = _exact_dot01(a_cum, r128_ref[...])
    alast_e = acum_e[CHUNK - 1:CHUNK, :]

    xdt = xs * dt_e
    xdt_b = xdt.astype(BF16)
    xw_b = (xdt * jnp.exp(alast_e - acum_e)).astype(BF16)
    chunk_decay = jnp.exp(alast_e)
    bm_b = bm.astype(BF16)
    cm_b = cm.astype(BF16)
    dn_nt = (((1,), (1,)), ((), ()))
    heads_per_group = C_HEADS // C_GROUPS
    gw = heads_per_group * C_HD

    y_parts = []
    for g in range(C_GROUPS):
        bg = bm[:, g * C_STATE:(g + 1) * C_STATE]
        bg_b = bm_b[:, g * C_STATE:(g + 1) * C_STATE]
        cg_b = cm_b[:, g * C_STATE:(g + 1) * C_STATE]
        cb = lax.dot_general(cg_b, bg_b, dn_nt, preferred_element_type=F32)
        bgt_b = bg.T.astype(BF16)
        gcols = slice(g * gw, (g + 1) * gw)
        new_states = jnp.dot(bgt_b, xw_b[:, gcols], preferred_element_type=F32)
        prev = state_ref[g]
        y_off = jnp.dot(cg_b, prev.astype(BF16), preferred_element_type=F32)
        state_ref[g] = prev * chunk_decay[:, gcols] + new_states
        diag = []
        for e in range(heads_per_group):
            hh = g * heads_per_group + e
            seg = acum_col[:, hh * LANES:(hh + 1) * LANES] - a_cum_t[hh:hh + 1, :]
            decay = jnp.exp(jnp.where(causal, seg, NEG))
            mh = (cb * decay).astype(BF16)
            diag.append(jnp.dot(mh, xdt_b[:, hh * C_HD:(hh + 1) * C_HD],
                                preferred_element_type=F32))
        y_diag = jnp.concatenate(diag, axis=1)
        y_g = (y_diag + y_off * jnp.exp(acum_e[:, gcols])
               + xs[:, gcols] * dexp_ref[:, gcols])
        zg = z_ref[:, gcols].astype(F32)
        yg = y_g * _silu(zg)
        ms = jnp.mean(yg * yg, axis=1, keepdims=True)
        y_parts.append(yg * lax.rsqrt(ms + NORM_EPS) * nw_ref[:, gcols])
    o_ref[...] = jnp.concatenate(y_parts, axis=1).astype(o_ref.dtype)


def _ssd(p3, p3b, dt3, conv_w, conv_b, dt_bias, a_log, d_skip, norm_w):
    b, s, _ = p3.shape
    nc = s // CHUNK
    pad = LANES - C_HEADS
    dtb = jnp.pad(dt_bias, (0, pad)).reshape(1, LANES)
    alog = jnp.pad(a_log, (0, pad)).reshape(1, LANES)
    dexp = jnp.repeat(d_skip, C_HD).reshape(1, BRANCH_WIDTH)
    heads = np.arange(LANES)[:, None]
    r64 = jnp.asarray(heads == (np.arange(C_HEADS * C_HD)[None, :] // C_HD), BF16)
    r128 = jnp.asarray(heads == (np.arange(C_HEADS * LANES)[None, :] // LANES), BF16)
    const = lambda bi, c: (0, 0)
    return pl.pallas_call(
        _ssd_kernel,
        grid=(b, nc),
        in_specs=[
            pl.BlockSpec((None, CHUNK, 512), lambda bi, c: (bi, c, COL_CXBC // 512)),
            pl.BlockSpec((None, CHUNK, 512), lambda bi, c: (bi, c, COL_CXBC // 512 + 1)),
            pl.BlockSpec((None, CHUNK, 512), lambda bi, c: (bi, c, COL_CXBC // 512 + 2)),
            pl.BlockSpec((None, CHUNK, 1024), lambda bi, c: (bi, c, COL2_CZ // 1024)),
            pl.BlockSpec((None, CHUNK, LANES), lambda bi, c: (bi, c, 0)),
            pl.BlockSpec((C_CONV, C_XBC), const),
            pl.BlockSpec((1, C_XBC), const),
            pl.BlockSpec((1, LANES), const),
            pl.BlockSpec((1, LANES), const),
            pl.BlockSpec((1, BRANCH_WIDTH), const),
            pl.BlockSpec((1, BRANCH_WIDTH), const),
            pl.BlockSpec((LANES, C_HEADS * C_HD), const),
            pl.BlockSpec((LANES, C_HEADS * LANES), const),
        ],
        out_specs=pl.BlockSpec((None, CHUNK, BRANCH_WIDTH), lambda bi, c: (bi, c, 0)),
        out_shape=jax.ShapeDtypeStruct((b, s, BRANCH_WIDTH), BF16),
        scratch_shapes=[pltpu.VMEM((CHUNK + 8, C_XBC), F32),
                        pltpu.VMEM((C_GROUPS, C_STATE, 512), F32)],
        compiler_params=_cparams("parallel", "arbitrary"),
        name="ssd",
    )(p3, p3, p3, p3b, dt3, conv_w, conv_b.reshape(1, C_XBC), dtb, alog, dexp,
      norm_w.reshape(1, BRANCH_WIDTH), r64, r128)


CONF_HALO = 32


CONF_ROWS_PER_ITER = 32
SUBLANES = 8


def _conformer_kernel(val_ref, gate_ref, g_ref, cw_ref, cb_ref, lnw_ref, lnb_ref, o_ref,
                      ext_ref, sh_ref, y_ref, *, ts):
    t = pl.program_id(1)

    @pl.when(t == 0)
    def _():
        ext_ref[0:CONF_HALO, :] = jnp.zeros((CONF_HALO, BRANCH_WIDTH), F32)

    ext_ref[CONF_HALO:CONF_HALO + ts, :] = (val_ref[...].astype(F32)
                                            * _sigmoid(gate_ref[...].astype(F32)))
    span = ts + CONF_HALO - SUBLANES
    for r in range(1, SUBLANES):
        sh_ref[r - 1] = ext_ref[r:r + span, :]

    first = CONF_HALO - (D_CONV - 1)
    for c in range(BRANCH_WIDTH // LANES):
        cols = slice(c * LANES, (c + 1) * LANES)
        taps = [jnp.broadcast_to(cw_ref[j:j + 1, cols], (SUBLANES, LANES)) for j in range(D_CONV)]
        bias = jnp.broadcast_to(cb_ref[:, cols], (SUBLANES, LANES))

        def body(it, carry):
            base = pl.multiple_of(it * CONF_ROWS_PER_ITER, CONF_ROWS_PER_ITER)
            for u in range(CONF_ROWS_PER_ITER // SUBLANES):
                parts = [bias, None]
                for j in range(D_CONV):
                    r = (first + j) % SUBLANES
                    row = base + u * SUBLANES + (first + j) - r
                    src = ext_ref if r == 0 else sh_ref.at[r - 1]
                    term = taps[j] * src[pl.ds(row, SUBLANES), cols]
                    parts[j % 2] = term if parts[j % 2] is None else parts[j % 2] + term
                y_ref[pl.ds(base + u * SUBLANES, SUBLANES), cols] = parts[0] + parts[1]
            return carry

        lax.fori_loop(0, ts // CONF_ROWS_PER_ITER, body, 0)

    ext_ref[0:CONF_HALO, :] = ext_ref[ts:ts + CONF_HALO, :]
    acc = y_ref[...]
    mu = jnp.mean(acc, axis=1, keepdims=True)
    xc = acc - mu
    var = jnp.mean(xc * xc, axis=1, keepdims=True)
    y = xc * lax.rsqrt(var + NORM_EPS) * lnw_ref[...] + lnb_ref[...]
    o_ref[...] = (_silu(y) * _silu(g_ref[...].astype(F32))).astype(o_ref.dtype)


def _conformer(p3b, conv_w, conv_b, ln_w, ln_b, ts):
    b, s, _ = p3b.shape
    ts = min(ts, s)
    const = lambda bi, t: (0, 0)
    w = BRANCH_WIDTH
    return pl.pallas_call(
        functools.partial(_conformer_kernel, ts=ts),
        grid=(b, s // ts),
        in_specs=[
            pl.BlockSpec((None, ts, w), lambda bi, t: (bi, t, COL2_DGLU // w)),
            pl.BlockSpec((None, ts, w), lambda bi, t: (bi, t, COL2_DGLU // w + 1)),
            pl.BlockSpec((None, ts, w), lambda bi, t: (bi, t, COL2_DG // w)),
            pl.BlockSpec((D_CONV, w), const),
            pl.BlockSpec((1, w), const),
            pl.BlockSpec((1, w), const),
            pl.BlockSpec((1, w), const),
        ],
        out_specs=pl.BlockSpec((None, ts, w), lambda bi, t: (bi, t, 0)),
        out_shape=jax.ShapeDtypeStruct((b, s, w), BF16),
        scratch_shapes=[pltpu.VMEM((CONF_HALO + ts, w), F32),
                        pltpu.VMEM((SUBLANES - 1, CONF_HALO + ts - SUBLANES, w), F32),
                        pltpu.VMEM((ts, w), F32)],
        compiler_params=_cparams("parallel", "arbitrary"),
        name="conformer",
    )(p3b, p3b, p3b, conv_w, conv_b.reshape(1, w), ln_w.reshape(1, w), ln_b.reshape(1, w))


def _merge_kernel(h_ref, ya_ref, yb_ref, yc_ref, yd_ref, wg0, wg1, wg2, wg3, wb_ref, o_ref):
    h = h_ref[...]
    acc = None
    for i, (y_ref, wg_ref) in enumerate(((ya_ref, wg0), (yb_ref, wg1), (yc_ref, wg2), (yd_ref, wg3))):
        gate = jnp.dot(h, wg_ref[...], preferred_element_type=F32)
        up = jnp.dot(y_ref[...], wb_ref[i], preferred_element_type=F32)
        term = _sigmoid(gate) * up
        acc = term if acc is None else acc + term
    o_ref[...] = acc.astype(o_ref.dtype)


def _merge(h2, ys, wg, wb, layer, tm, tn):
    t, d = h2.shape
    tm = min(tm, t)
    nblk = d // tn
    y_spec = pl.BlockSpec((tm, BRANCH_WIDTH), lambda i, j: (i, 0))
    wg_specs = [pl.BlockSpec((None, d, tn),
                             functools.partial(lambda i, j, k: (layer, 0, k * nblk + j), k=k))
                for k in range(N_BRANCH)]
    return pl.pallas_call(
        _merge_kernel,
        grid=(t // tm, nblk),
        in_specs=[pl.BlockSpec((tm, d), lambda i, j: (i, 0)), y_spec, y_spec, y_spec, y_spec,
                  *wg_specs,
                  pl.BlockSpec((None, N_BRANCH, BRANCH_WIDTH, tn), lambda i, j: (layer, 0, 0, j))],
        out_specs=pl.BlockSpec((tm, tn), lambda i, j: (i, j)),
        out_shape=jax.ShapeDtypeStruct((t, d), BF16),
        compiler_params=_cparams("parallel", "parallel"),
        name="merge",
    )(h2, *ys, wg, wg, wg, wg, wb)


def _outproj_kernel(m_ref, w_ref, x_ref, nw_ref, *out_refs, last):
    xn = x_ref[...] + jnp.dot(m_ref[...], w_ref[...], preferred_element_type=F32)
    ms = jnp.mean(xn * xn, axis=-1, keepdims=True)
    normed = xn * lax.rsqrt(ms + NORM_EPS) * nw_ref[...]
    if last:
        out_refs[0][...] = normed
    else:
        out_refs[0][...] = xn
        out_refs[1][...] = normed.astype(BF16)


def _outproj(merged, w_out, layer, x2, next_norm_w, last, tm):
    t, d = x2.shape
    tm = min(tm, t)
    row = pl.BlockSpec((tm, d), lambda i: (i, 0))
    if last:
        out_specs = row
        out_shape = jax.ShapeDtypeStruct((t, d), F32)
    else:
        out_specs = (row, row)
        out_shape = (jax.ShapeDtypeStruct((t, d), F32), jax.ShapeDtypeStruct((t, d), BF16))
    return pl.pallas_call(
        functools.partial(_outproj_kernel, last=last),
        grid=(t // tm,),
        in_specs=[row, pl.BlockSpec((None, d, d), lambda i: (layer, 0, 0)), row,
                  pl.BlockSpec((1, d), lambda i: (0, 0))],
        out_specs=out_specs,
        out_shape=out_shape,
        compiler_params=_cparams("parallel"),
        name="outproj",
    )(merged, w_out, x2, next_norm_w.reshape(1, d))


def kernel(x, norm_w, w_in, diff_lambda, diff_subln_w, swa_sinks, ssd_conv_w, ssd_conv_b,
           ssd_dt_bias, ssd_a_log, ssd_d, ssd_norm_w, conf_conv_w, conf_conv_b, conf_ln_w,
           conf_ln_b, w_branch, w_out, rel_bias, final_norm_w):
    b, s, d = x.shape
    depth = w_in.shape[0]
    t = b * s
    tq = min(256, s)
    bias_a = _bias_tiles(rel_bias, tq, 0, A_HEADS, None)
    bias_b = _bias_tiles(rel_bias, WINDOW, A_HEADS, B_HEADS, WINDOW)

    first_scale = np.ones((1, N_FIRST), np.float32)
    first_scale[0, :A_HEADS * 2 * A_QK] = A_QK ** -0.5 * LOG2E
    first_scale[0, COL_BQ:COL_BQ + B_HEADS * B_HD] = B_HD ** -0.5 * LOG2E
    first_scale = jnp.asarray(first_scale)
    ones = lambda n: jnp.ones((1, n), F32)

    w_first = w_in[:, :, :N_FIRST].astype(BF16)
    w_second = w_in[:, :, SRC_SECOND[0]:SRC_SECOND[1]].astype(BF16)
    w_dt = jnp.pad(w_in[:, :, SRC_DT[0]:SRC_DT[1]],
                   ((0, 0), (0, 0), (0, LANES - C_HEADS))).astype(BF16)
    w_gate = w_in[:, :, SRC_MG:].astype(BF16)
    w_br = w_branch.astype(BF16)
    w_o = w_out.astype(BF16)

    x2 = x.reshape(t, d)
    h2 = _rmsnorm(x2, norm_w[0], BF16)
    out = None
    for l in range(depth):
        first = _matmul(h2, w_first, l, first_scale, BF16, 1024, 1024, "inproj_first")
        second = _matmul(h2, w_second, l, ones(SRC_SECOND[1] - SRC_SECOND[0]), BF16, 1024, 1024,
                         "inproj_second")
        dt_raw = _matmul(h2, w_dt, l, ones(LANES), F32, 1024, LANES, "dtproj")
        p3 = first.reshape(b, s, N_FIRST)
        p3b = second.reshape(b, s, SRC_SECOND[1] - SRC_SECOND[0])
        dt3 = dt_raw.reshape(b, s, LANES)

        lam_init = 0.8 - 0.6 * math.exp(-0.3 * l)
        ya = _diff_attention(p3, diff_lambda[l], diff_subln_w[l], rel_bias, bias_a, lam_init, tq)
        yb = _swa(p3, swa_sinks[l], bias_b)
        yc = _ssd(p3, p3b, dt3, ssd_conv_w[l], ssd_conv_b[l], ssd_dt_bias[l], ssd_a_log[l],
                  ssd_d[l], ssd_norm_w[l])
        yd = _conformer(p3b, conf_conv_w[l], conf_conv_b[l], conf_ln_w[l], conf_ln_b[l], 256)

        ys = [y.reshape(t, BRANCH_WIDTH) for y in (ya, yb, yc, yd)]
        merged = _merge(h2, ys, w_gate, w_br, l, 1024, 256)
        last = l == depth - 1
        next_w = final_norm_w if last else norm_w[l + 1]
        res = _outproj(merged, w_o, l, x2, next_w, last, 512)
        if last:
            out = res
        else:
            x2, h2 = res
    return out.reshape(b, s, d)
```

```python
import functools
import math

import jax
import jax.numpy as jnp
import numpy as np
from jax import lax
from jax.experimental import pallas as pl
from jax.experimental.pallas import tpu as pltpu

F32 = jnp.float32
BF16 = jnp.bfloat16

D_MODEL = 2048
NORM_EPS = 1e-6
BRANCH_WIDTH = 1024
NUM_BUCKETS = 32
MAX_DISTANCE = 128

A_QK = 64
A_V = 128
A_HEADS = 8
B_HD = 64
B_HEADS = 16
B_KV = 4
WINDOW = 128
C_HEADS = 16
C_HD = 64
C_GROUPS = 2
C_STATE = 128
C_CONV = 4
CHUNK = 128
C_XBC = 1536
D_CONV = 31
N_BRANCH = 4

N_FIRST = 8192
COL_BQ = 4096
COL_BK = 5120
COL_BV = 5376
COL_BG = 5632
COL_CXBC = 6656
SRC_DT = (8192, 8208)
SRC_SECOND = (8208, 12304)
COL2_CZ = 0
COL2_DGLU = 1024
COL2_DG = 3072
SRC_MG = 12304

NEG = -1e30
LOG2E = 1.4426950408889634
LANES = 128
VMEM_LIMIT = 56 * 1024 * 1024


def _cparams(*sem):
    return pltpu.CompilerParams(dimension_semantics=sem, vmem_limit_bytes=VMEM_LIMIT)


def _sigmoid(x):
    return 1.0 / (1.0 + jnp.exp(-x))


def _silu(x):
    return x * _sigmoid(x)


def _rmsnorm_kernel(x_ref, w_ref, o_ref):
    x = x_ref[...]
    ms = jnp.mean(x * x, axis=-1, keepdims=True)
    o_ref[...] = (x * lax.rsqrt(ms + NORM_EPS) * w_ref[...]).astype(o_ref.dtype)


def _rmsnorm(x2, w, out_dtype):
    t, d = x2.shape
    tm = min(512, t)
    return pl.pallas_call(
        _rmsnorm_kernel,
        grid=(t // tm,),
        in_specs=[pl.BlockSpec((tm, d), lambda i: (i, 0)),
                  pl.BlockSpec((1, d), lambda i: (0, 0))],
        out_specs=pl.BlockSpec((tm, d), lambda i: (i, 0)),
        out_shape=jax.ShapeDtypeStruct((t, d), out_dtype),
        compiler_params=_cparams("parallel"),
        name="rmsnorm",
    )(x2, w.reshape(1, d))


def _matmul_kernel(a_ref, b_ref, cs_ref, o_ref):
    acc = jnp.dot(a_ref[...], b_ref[...], preferred_element_type=F32)
    o_ref[...] = (acc * cs_ref[...]).astype(o_ref.dtype)


def _matmul(a, b, layer, col_scale, out_dtype, tm, tn, name):
    m, k = a.shape
    n = col_scale.shape[1]
    tm = min(tm, m)
    tn = min(tn, n)
    return pl.pallas_call(
        _matmul_kernel,
        grid=(m // tm, n // tn),
        in_specs=[pl.BlockSpec((tm, k), lambda i, j: (i, 0)),
                  pl.BlockSpec((None, k, tn), lambda i, j: (layer, 0, j)),
                  pl.BlockSpec((1, tn), lambda i, j: (0, j))],
        out_specs=pl.BlockSpec((tm, tn), lambda i, j: (i, j)),
        out_shape=jax.ShapeDtypeStruct((m, n), out_dtype),
        compiler_params=_cparams("parallel", "parallel"),
        name=name,
    )(a, b, col_scale)


def _shift_cast_kernel(a_ref, b_ref, o_ref, *, shift):
    x = jnp.concatenate([a_ref[...], b_ref[...]], axis=1)
    o_ref[...] = x[:, shift:shift + o_ref.shape[1]].astype(o_ref.dtype)


def _shift_cast(w, col0, ncols, tn):
    depth, k, _ = w.shape
    base = col0 // LANES * LANES
    assert base % tn == 0 and ncols % tn == 0 and tn % LANES == 0
    return pl.pallas_call(
        functools.partial(_shift_cast_kernel, shift=col0 - base),
        grid=(depth, ncols // tn),
        in_specs=[pl.BlockSpec((None, k, tn), lambda l, j: (l, 0, base // tn + j)),
                  pl.BlockSpec((None, k, LANES),
                               lambda l, j: (l, 0, (base + (j + 1) * tn) // LANES))],
        out_specs=pl.BlockSpec((None, k, tn), lambda l, j: (l, 0, j)),
        out_shape=jax.ShapeDtypeStruct((depth, k, ncols), BF16),
        compiler_params=_cparams("parallel", "parallel"),
        name="shift_cast",
    )(w, w)


def _dt_weights_kernel(a_ref, o_ref):
    lane = lax.broadcasted_iota(jnp.int32, a_ref.shape, 1)
    o_ref[...] = jnp.where(lane < C_HEADS, a_ref[...], 0.0).astype(o_ref.dtype)


def _dt_weights(w):
    depth, k, _ = w.shape
    assert SRC_DT[0] % LANES == 0 and SRC_DT[1] - SRC_DT[0] == C_HEADS
    return pl.pallas_call(
        _dt_weights_kernel,
        grid=(depth,),
        in_specs=[pl.BlockSpec((None, k, LANES), lambda l: (l, 0, SRC_DT[0] // LANES))],
        out_specs=pl.BlockSpec((None, k, LANES), lambda l: (l, 0, 0)),
        out_shape=jax.ShapeDtypeStruct((depth, k, LANES), BF16),
        compiler_params=_cparams("parallel"),
        name="dt_weights",
    )(w)


def _bucket_lower_bounds():
    max_exact = NUM_BUCKETS // 2
    lower = list(range(max_exact))
    for k in range(max_exact, NUM_BUCKETS):
        d = max_exact
        while True:
            large = max_exact + int(math.log(d / max_exact) / math.log(MAX_DISTANCE / max_exact)
                                    * (NUM_BUCKETS - max_exact))
            if min(large, NUM_BUCKETS - 1) >= k:
                break
            d += 1
        lower.append(d)
    return tuple(lower)


BUCKET_LOWER = _bucket_lower_bounds()


def _bias_of_distance(d, tbl_ref, h):
    val = jnp.zeros(d.shape, F32) + tbl_ref[0, h] * LOG2E
    for k in range(1, NUM_BUCKETS):
        val = jnp.where(d >= BUCKET_LOWER[k], tbl_ref[k, h] * LOG2E, val)
    return val


def _bias_tiles_kernel(tbl_ref, o_ref, *, n, head0, window):
    h = pl.program_id(0) + head0
    row = lax.broadcasted_iota(jnp.int32, (n, n), 0)
    col = lax.broadcasted_iota(jnp.int32, (n, n), 1)
    d_prev = row - col + n
    d_cur = row - col
    prev = _bias_of_distance(d_prev, tbl_ref, h)
    if window is not None:
        prev = jnp.where(d_prev < window, prev, NEG)
    o_ref[:, :n] = prev
    o_ref[:, n:] = jnp.where(d_cur >= 0, _bias_of_distance(jnp.maximum(d_cur, 0), tbl_ref, h), NEG)


def _bias_tiles(rel_bias, n, head0, heads, window):
    return pl.pallas_call(
        functools.partial(_bias_tiles_kernel, n=n, head0=head0, window=window),
        grid=(heads,),
        in_specs=[pl.BlockSpec(memory_space=pltpu.SMEM)],
        out_specs=pl.BlockSpec((None, n, 2 * n), lambda h: (h, 0, 0)),
        out_shape=jax.ShapeDtypeStruct((heads, n, 2 * n), F32),
        compiler_params=_cparams("parallel"),
        name="bias_tiles",
    )(rel_bias)


def _lane_fold(x, op):
    r = x[:, :LANES]
    for c in range(1, x.shape[1] // LANES):
        r = op(r, x[:, c * LANES:(c + 1) * LANES])
    return r


def _diffattn_kernel(tbl_ref, lam_ref, q_ref, k_ref, v_ref, g_ref, bias_ref, subw_ref,
                     o_ref, s_ref, *, tq, nq, lam_init):
    h = pl.program_id(1)
    reps = tq // LANES
    dn_nt = (((1,), (1,)), ((), ()))
    cfar = tbl_ref[NUM_BUCKETS - 1, h] * LOG2E
    lp = lam_ref[...]
    lam = (jnp.exp(jnp.sum(lp[0:1] * lp[1:2], axis=1, keepdims=True))
           - jnp.exp(jnp.sum(lp[2:3] * lp[3:4], axis=1, keepdims=True)) + lam_init)
    lane = lax.broadcasted_iota(jnp.int32, (tq, LANES), 1)

    def with_bias(s, b):
        return jnp.concatenate([s[:tq] + b, s[tq:] + b], axis=0)

    for i in range(nq):
        rows = slice(i * tq, (i + 1) * tq)
        q = q_ref[rows, :]
        zero = jnp.zeros_like(q)
        qz = jnp.concatenate([jnp.where(lane < A_QK, q, zero),
                              jnp.where(lane >= A_QK, q, zero)], axis=0)
        slot = i % 2
        mp_far = None
        mp_near = None
        for j in range(i + 1):
            s = lax.dot_general(qz, k_ref[j * tq:(j + 1) * tq, :], dn_nt,
                                preferred_element_type=F32)
            if j == i:
                s = with_bias(s, bias_ref[:, tq:])
            elif j == i - 1:
                s = with_bias(s, bias_ref[:, :tq])
            s_ref[slot, j] = s
            f = _lane_fold(s, jnp.maximum)
            if j >= i - 1:
                mp_near = f if mp_near is None else jnp.maximum(mp_near, f)
            else:
                mp_far = f if mp_far is None else jnp.maximum(mp_far, f)
        mp = mp_near if mp_far is None else jnp.maximum(mp_near, mp_far + cfar)
        m = jnp.broadcast_to(jnp.max(mp, axis=1, keepdims=True), mp.shape)
        m_near = jnp.concatenate([m] * reps, axis=1)
        m_far = jnp.concatenate([m - cfar] * reps, axis=1)
        lpart = None
        acc = None
        for j in range(i + 1):
            p = jnp.exp2(s_ref[slot, j] - (m_near if j >= i - 1 else m_far))
            f = _lane_fold(p, jnp.add)
            pv = jnp.dot(p.astype(BF16), v_ref[j * tq:(j + 1) * tq, :],
                         preferred_element_type=F32)
            lpart = f if lpart is None else lpart + f
            acc = pv if acc is None else acc + pv
        lsum = jnp.broadcast_to(jnp.sum(lpart, axis=1, keepdims=True), lpart.shape)
        o_all = acc / lsum
        o = o_all[:tq] - lam * o_all[tq:]
        ms = jnp.mean(o * o, axis=1, keepdims=True)
        y = o * lax.rsqrt(ms + NORM_EPS) * subw_ref[...] * (1.0 - lam_init)
        o_ref[rows, :] = (y * _silu(g_ref[rows, :].astype(F32))).astype(o_ref.dtype)


def _diff_attention(p3, lam_params, subln_w, rel_bias, bias, lam_init, tq):
    b, s, _ = p3.shape
    nq = s // tq
    assert tq + 1 >= BUCKET_LOWER[-1]

    kernel = functools.partial(_diffattn_kernel, tq=tq, nq=nq, lam_init=lam_init)
    col = lambda off: (lambda bi, h: (bi, 0, off + h))
    return pl.pallas_call(
        kernel,
        grid=(b, A_HEADS),
        in_specs=[
            pl.BlockSpec(memory_space=pltpu.SMEM),
            pl.BlockSpec((4, A_QK), lambda bi, h: (0, 0)),
            pl.BlockSpec((None, s, LANES), col(0)),
            pl.BlockSpec((None, s, LANES), col(A_HEADS)),
            pl.BlockSpec((None, s, LANES), col(2 * A_HEADS)),
            pl.BlockSpec((None, s, LANES), col(3 * A_HEADS)),
            pl.BlockSpec((None, tq, 2 * tq), lambda bi, h: (h, 0, 0)),
            pl.BlockSpec((1, A_V), lambda bi, h: (0, 0)),
        ],
        out_specs=pl.BlockSpec((None, s, LANES), lambda bi, h: (bi, 0, h)),
        out_shape=jax.ShapeDtypeStruct((b, s, BRANCH_WIDTH), BF16),
        scratch_shapes=[pltpu.VMEM((2, nq, 2 * tq, tq), F32)],
        compiler_params=_cparams("parallel", "parallel"),
        name="diff_attention",
    )(rel_bias, lam_params, p3, p3, p3, p3, bias, subln_w.reshape(1, A_V))


def _swa_kernel(sink_ref, q_ref, glo_ref, ghi_ref, kc_ref, kp_ref, vc_ref, vp_ref, bias_ref, o_ref):
    n = pl.program_id(1)
    w = WINDOW
    dn_nt = (((1,), (1,)), ((), ()))
    group = B_HEADS // B_KV
    low = lax.broadcasted_iota(jnp.int32, (w, LANES), 1) < B_HD
    low2 = lax.broadcasted_iota(jnp.int32, (2 * w, LANES), 1) < B_HD
    key_col = lax.broadcasted_iota(jnp.int32, (1, 2 * w), 1)
    pen = jnp.where(key_col < w, jnp.where(n > 0, 0.0, NEG), 0.0).astype(F32)
    kcat = jnp.concatenate([kp_ref[...], kc_ref[...]], axis=0)
    vcat = jnp.concatenate([vp_ref[...], vc_ref[...]], axis=0)

    def dup_halves(x):
        xf = x.astype(F32)
        rolled = pltpu.roll(xf, B_HD, 1)
        return (jnp.where(low2, xf, rolled).astype(BF16),
                jnp.where(low2, rolled, xf).astype(BF16))

    for kvcol in range(B_KV * B_HD // LANES):
        kz = dup_halves(kcat[:, kvcol * LANES:(kvcol + 1) * LANES])
        vz = dup_halves(vcat[:, kvcol * LANES:(kvcol + 1) * LANES])
        for half in range(2):
            kv = 2 * kvcol + half
            qcols = [2 * kv + c for c in range(group // 2)]
            rows = []
            for c in qcols:
                qc = q_ref[:, c * LANES:(c + 1) * LANES]
                zero = jnp.zeros_like(qc)
                rows += [jnp.where(low, qc, zero), jnp.where(low, zero, qc)]
            qz = jnp.concatenate(rows, axis=0)
            s = lax.dot_general(qz, kz[half], dn_nt, preferred_element_type=F32)
            s = s + bias_ref[kv * group:(kv + 1) * group].reshape(group * w, 2 * w) + pen
            sink = jnp.concatenate(
                [jnp.zeros((w, LANES), F32) + sink_ref[kv * group + r] * LOG2E for r in range(group)],
                axis=0)
            m = jnp.max(_lane_fold(s, jnp.maximum), axis=1, keepdims=True)
            m = jnp.maximum(jnp.broadcast_to(m, sink.shape), sink)
            p = jnp.exp2(s - jnp.concatenate([m] * (2 * w // LANES), axis=1))
            l = jnp.sum(_lane_fold(p, jnp.add), axis=1, keepdims=True)
            l = jnp.broadcast_to(l, sink.shape) + jnp.exp2(sink - m)
            o = jnp.dot(p.astype(BF16), vz[half], preferred_element_type=F32) / l
            for ci, c in enumerate(qcols):
                pair = jnp.where(low, o[2 * ci * w:(2 * ci + 1) * w], o[(2 * ci + 1) * w:(2 * ci + 2) * w])
                cols = slice(c * LANES, (c + 1) * LANES)
                half_cols = BRANCH_WIDTH // 2 // LANES
                g_ref = glo_ref if c < half_cols else ghi_ref
                gcols = slice((c % half_cols) * LANES, (c % half_cols + 1) * LANES)
                o_ref[:, cols] = (pair * _silu(g_ref[:, gcols].astype(F32))).astype(o_ref.dtype)


def _swa(p3, sinks, bias):
    b, s, _ = p3.shape
    nb = s // WINDOW

    w = WINDOW
    prev_idx = lambda bi, n, c: (bi, jnp.maximum(n - 1, 0), c)
    return pl.pallas_call(
        _swa_kernel,
        grid=(b, nb),
        in_specs=[
            pl.BlockSpec(memory_space=pltpu.SMEM),
            pl.BlockSpec((None, w, 1024), lambda bi, n: (bi, n, COL_BQ // 1024)),
            pl.BlockSpec((None, w, 512), lambda bi, n: (bi, n, COL_BG // 512)),
            pl.BlockSpec((None, w, 512), lambda bi, n: (bi, n, COL_BG // 512 + 1)),
            pl.BlockSpec((None, w, 256), lambda bi, n: (bi, n, COL_BK // 256)),
            pl.BlockSpec((None, w, 256), lambda bi, n: prev_idx(bi, n, COL_BK // 256)),
            pl.BlockSpec((None, w, 256), lambda bi, n: (bi, n, COL_BV // 256)),
            pl.BlockSpec((None, w, 256), lambda bi, n: prev_idx(bi, n, COL_BV // 256)),
            pl.BlockSpec((B_HEADS, w, 2 * w), lambda bi, n: (0, 0, 0)),
        ],
        out_specs=pl.BlockSpec((None, w, 1024), lambda bi, n: (bi, n, 0)),
        out_shape=jax.ShapeDtypeStruct((b, s, BRANCH_WIDTH), BF16),
        compiler_params=_cparams("parallel", "parallel"),
        name="swa",
    )(sinks, p3, p3, p3, p3, p3, p3, p3, bias)


def _exact_dot01(v, r):
    hi = v.astype(BF16)
    r1 = v - hi.astype(F32)
    mid = r1.astype(BF16)
    lo = (r1 - mid.astype(F32)).astype(BF16)
    return (jnp.dot(hi, r, preferred_element_type=F32)
            + jnp.dot(mid, r, preferred_element_type=F32)
            + jnp.dot(lo, r, preferred_element_type=F32))


def _ssd_kernel(xbc0_ref, xbc1_ref, xbc2_ref, z_ref, dt_ref, cw_ref, cb_ref, dtb_ref, alog_ref,
                dexp_ref, nw_ref, r64_ref, r128_ref, o_ref, ext_ref, state_ref):
    c = pl.program_id(1)
    halo = 8

    @pl.when(c == 0)
    def _():
        ext_ref[0:halo, :] = jnp.zeros((halo, C_XBC), F32)
        state_ref[...] = jnp.zeros(state_ref.shape, F32)

    xraw = jnp.concatenate([xbc0_ref[...], xbc1_ref[...], xbc2_ref[...]], axis=1).astype(F32)
    ext_ref[halo:halo + CHUNK, :] = xraw
    acc = jnp.broadcast_to(cb_ref[...], (CHUNK, C_XBC))
    for j in range(C_CONV):
        off = halo - (C_CONV - 1) + j
        acc = acc + cw_ref[j:j + 1, :] * ext_ref[off:off + CHUNK, :]
    ext_ref[0:halo, :] = xraw[CHUNK - halo:, :]
    xc = _silu(acc)
    xs = xc[:, :1024]
    bm = xc[:, 1024:1024 + C_GROUPS * C_STATE]
    cm = xc[:, 1024 + C_GROUPS * C_STATE:]

    dtx = dt_ref[...] + dtb_ref[...]
    dt = jnp.maximum(dtx, 0.0) + jnp.log1p(jnp.exp(-jnp.abs(dtx)))
    da = dt * (-jnp.exp(alog_ref[...]))
    row = lax.broadcasted_iota(jnp.int32, (CHUNK, CHUNK), 0)
    col = lax.broadcasted_iota(jnp.int32, (CHUNK, CHUNK), 1)
    causal = row >= col
    tri = causal.astype(BF16)
    hi = da.astype(BF16)
    r1 = da - hi.astype(F32)
    mid = r1.astype(BF16)
    lo = (r1 - mid.astype(F32)).astype(BF16)
    a_cum = (jnp.dot(tri, hi, preferred_element_type=F32)
             + jnp.dot(tri, mid, preferred_element_type=F32)
             + jnp.dot(tri, lo, preferred_element_type=F32))
    a_cum_t = a_cum.T

    r64 = r64_ref[...]
    dt_e = _exact_dot01(dt, r64)
    acum_e = _exact_dot01(a_cum, r64)
    acum_col = _exact_dot01(a_cum, r128_ref[...])
    alast_e = acum_e[CHUNK - 1:CHUNK, :]

    xdt = xs * dt_e
    xdt_b = xdt.astype(BF16)
    xw_b = (xdt * jnp.exp(alast_e - acum_e)).astype(BF16)
    chunk_decay = jnp.exp(alast_e)
    bm_b = bm.astype(BF16)
    cm_b = cm.astype(BF16)
    dn_nt = (((1,), (1,)), ((), ()))
    heads_per_group = C_HEADS // C_GROUPS
    gw = heads_per_group * C_HD

    y_parts = []
    for g in range(C_GROUPS):
        bg = bm[:, g * C_STATE:(g + 1) * C_STATE]
        bg_b = bm_b[:, g * C_STATE:(g + 1) * C_STATE]
        cg_b = cm_b[:, g * C_STATE:(g + 1) * C_STATE]
        cb = lax.dot_general(cg_b, bg_b, dn_nt, preferred_element_type=F32)
        bgt_b = bg.T.astype(BF16)
        gcols = slice(g * gw, (g + 1) * gw)
        new_states = jnp.dot(bgt_b, xw_b[:, gcols], preferred_element_type=F32)
        prev = state_ref[g]
        y_off = jnp.dot(cg_b, prev.astype(BF16), preferred_element_type=F32)
        state_ref[g] = prev * chunk_decay[:, gcols] + new_states
        diag = []
        for e in range(heads_per_group):
            hh = g * heads_per_group + e
            seg = acum_col[:, hh * LANES:(hh + 1) * LANES] - a_cum_t[hh:hh + 1, :]
            decay = jnp.exp(jnp.where(causal, seg, NEG))
            mh = (cb * decay).astype(BF16)
            diag.append(jnp.dot(mh, xdt_b[:, hh * C_HD:(hh + 1) * C_HD],
                                preferred_element_type=F32))
        y_diag = jnp.concatenate(diag, axis=1)
        y_g = (y_diag + y_off * jnp.exp(acum_e[:, gcols])
               + xs[:, gcols] * dexp_ref[:, gcols])
        zg = z_ref[:, gcols].astype(F32)
        yg = y_g * _silu(zg)
        ms = jnp.mean(yg * yg, axis=1, keepdims=True)
        y_parts.append(yg * lax.rsqrt(ms + NORM_EPS) * nw_ref[:, gcols])
    o_ref[...] = jnp.concatenate(y_parts, axis=1).astype(o_ref.dtype)


def _ssd(p3, p3b, dt3, conv_w, conv_b, dt_bias, a_log, d_skip, norm_w):
    b, s, _ = p3.shape
    nc = s // CHUNK
    pad = LANES - C_HEADS
    dtb = jnp.pad(dt_bias, (0, pad)).reshape(1, LANES)
    alog = jnp.pad(a_log, (0, pad)).reshape(1, LANES)
    dexp = jnp.repeat(d_skip, C_HD).reshape(1, BRANCH_WIDTH)
    heads = np.arange(LANES)[:, None]
    r64 = jnp.asarray(heads == (np.arange(C_HEADS * C_HD)[None, :] // C_HD), BF16)
    r128 = jnp.asarray(heads == (np.arange(C_HEADS * LANES)[None, :] // LANES), BF16)
    const = lambda bi, c: (0, 0)
    return pl.pallas_call(
        _ssd_kernel,
        grid=(b, nc),
        in_specs=[
            pl.BlockSpec((None, CHUNK, 512), lambda bi, c: (bi, c, COL_CXBC // 512)),
            pl.BlockSpec((None, CHUNK, 512), lambda bi, c: (bi, c, COL_CXBC // 512 + 1)),
            pl.BlockSpec((None, CHUNK, 512), lambda bi, c: (bi, c, COL_CXBC // 512 + 2)),
            pl.BlockSpec((None, CHUNK, 1024), lambda bi, c: (bi, c, COL2_CZ // 1024)),
            pl.BlockSpec((None, CHUNK, LANES), lambda bi, c: (bi, c, 0)),
            pl.BlockSpec((C_CONV, C_XBC), const),
            pl.BlockSpec((1, C_XBC), const),
            pl.BlockSpec((1, LANES), const),
            pl.BlockSpec((1, LANES), const),
            pl.BlockSpec((1, BRANCH_WIDTH), const),
            pl.BlockSpec((1, BRANCH_WIDTH), const),
            pl.BlockSpec((LANES, C_HEADS * C_HD), const),
            pl.BlockSpec((LANES, C_HEADS * LANES), const),
        ],
        out_specs=pl.BlockSpec((None, CHUNK, BRANCH_WIDTH), lambda bi, c: (bi, c, 0)),
        out_shape=jax.ShapeDtypeStruct((b, s, BRANCH_WIDTH), BF16),
        scratch_shapes=[pltpu.VMEM((CHUNK + 8, C_XBC), F32),
                        pltpu.VMEM((C_GROUPS, C_STATE, 512), F32)],
        compiler_params=_cparams("parallel", "arbitrary"),
        name="ssd",
    )(p3, p3, p3, p3b, dt3, conv_w, conv_b.reshape(1, C_XBC), dtb, alog, dexp,
      norm_w.reshape(1, BRANCH_WIDTH), r64, r128)


CONF_HALO = 32


CONF_ROWS_PER_ITER = 32
SUBLANES = 8


def _conformer_kernel(val_ref, gate_ref, g_ref, cw_ref, cb_ref, lnw_ref, lnb_ref, o_ref,
                      ext_ref, sh_ref, y_ref, *, ts):
    t = pl.program_id(1)

    @pl.when(t == 0)
    def _():
        ext_ref[0:CONF_HALO, :] = jnp.zeros((CONF_HALO, BRANCH_WIDTH), F32)

    ext_ref[CONF_HALO:CONF_HALO + ts, :] = (val_ref[...].astype(F32)
                                            * _sigmoid(gate_ref[...].astype(F32)))
    span = ts + CONF_HALO - SUBLANES
    for r in range(1, SUBLANES):
        sh_ref[r - 1] = ext_ref[r:r + span, :]

    first = CONF_HALO - (D_CONV - 1)
    for c in range(BRANCH_WIDTH // LANES):
        cols = slice(c * LANES, (c + 1) * LANES)
        taps = [jnp.broadcast_to(cw_ref[j:j + 1, cols], (SUBLANES, LANES)) for j in range(D_CONV)]
        bias = jnp.broadcast_to(cb_ref[:, cols], (SUBLANES, LANES))

        def body(it, carry):
            base = pl.multiple_of(it * CONF_ROWS_PER_ITER, CONF_ROWS_PER_ITER)
            for u in range(CONF_ROWS_PER_ITER // SUBLANES):
                parts = [bias, None]
                for j in range(D_CONV):
                    r = (first + j) % SUBLANES
                    row = base + u * SUBLANES + (first + j) - r
                    src = ext_ref if r == 0 else sh_ref.at[r - 1]
                    term = taps[j] * src[pl.ds(row, SUBLANES), cols]
                    parts[j % 2] = term if parts[j % 2] is None else parts[j % 2] + term
                y_ref[pl.ds(base + u * SUBLANES, SUBLANES), cols] = parts[0] + parts[1]
            return carry

        lax.fori_loop(0, ts // CONF_ROWS_PER_ITER, body, 0)

    ext_ref[0:CONF_HALO, :] = ext_ref[ts:ts + CONF_HALO, :]
    acc = y_ref[...]
    mu = jnp.mean(acc, axis=1, keepdims=True)
    xc = acc - mu
    var = jnp.mean(xc * xc, axis=1, keepdims=True)
    y = xc * lax.rsqrt(var + NORM_EPS) * lnw_ref[...] + lnb_ref[...]
    o_ref[...] = (_silu(y) * _silu(g_ref[...].astype(F32))).astype(o_ref.dtype)


def _conformer(p3b, conv_w, conv_b, ln_w, ln_b, ts):
    b, s, _ = p3b.shape
    ts = min(ts, s)
    const = lambda bi, t: (0, 0)
    w = BRANCH_WIDTH
    return pl.pallas_call(
        functools.partial(_conformer_kernel, ts=ts),
        grid=(b, s // ts),
        in_specs=[
            pl.BlockSpec((None, ts, w), lambda bi, t: (bi, t, COL2_DGLU // w)),
            pl.BlockSpec((None, ts, w), lambda bi, t: (bi, t, COL2_DGLU // w + 1)),
            pl.BlockSpec((None, ts, w), lambda bi, t: (bi, t, COL2_DG // w)),
            pl.BlockSpec((D_CONV, w), const),
            pl.BlockSpec((1, w), const),
            pl.BlockSpec((1, w), const),
            pl.BlockSpec((1, w), const),
        ],
        out_specs=pl.BlockSpec((None, ts, w), lambda bi, t: (bi, t, 0)),
        out_shape=jax.ShapeDtypeStruct((b, s, w), BF16),
        scratch_shapes=[pltpu.VMEM((CONF_HALO + ts, w), F32),
                        pltpu.VMEM((SUBLANES - 1, CONF_HALO + ts - SUBLANES, w), F32),
                        pltpu.VMEM((ts, w), F32)],
        compiler_params=_cparams("parallel", "arbitrary"),
        name="conformer",
    )(p3b, p3b, p3b, conv_w, conv_b.reshape(1, w), ln_w.reshape(1, w), ln_b.reshape(1, w))


def _merge_kernel(h_ref, ya_ref, yb_ref, yc_ref, yd_ref, wg0, wg1, wg2, wg3, wb_ref, o_ref):
    h = h_ref[...]
    acc = None
    for i, (y_ref, wg_ref) in enumerate(((ya_ref, wg0), (yb_ref, wg1), (yc_ref, wg2), (yd_ref, wg3))):
        gate = jnp.dot(h, wg_ref[...], preferred_element_type=F32)
        up = jnp.dot(y_ref[...], wb_ref[i], preferred_element_type=F32)
        term = _sigmoid(gate) * up
        acc = term if acc is None else acc + term
    o_ref[...] = acc.astype(o_ref.dtype)


def _merge(h2, ys, wg, wg_col0, wb, layer, tm, tn):
    t, d = h2.shape
    tm = min(tm, t)
    nblk = d // tn
    assert wg_col0 % tn == 0
    y_spec = pl.BlockSpec((tm, BRANCH_WIDTH), lambda i, j: (i, 0))
    wg_specs = [pl.BlockSpec((None, d, tn),
                             functools.partial(
                                 lambda i, j, k: (layer, 0, wg_col0 // tn + k * nblk + j), k=k))
                for k in range(N_BRANCH)]
    return pl.pallas_call(
        _merge_kernel,
        grid=(t // tm, nblk),
        in_specs=[pl.BlockSpec((tm, d), lambda i, j: (i, 0)), y_spec, y_spec, y_spec, y_spec,
                  *wg_specs,
                  pl.BlockSpec((None, N_BRANCH, BRANCH_WIDTH, tn), lambda i, j: (layer, 0, 0, j))],
        out_specs=pl.BlockSpec((tm, tn), lambda i, j: (i, j)),
        out_shape=jax.ShapeDtypeStruct((t, d), BF16),
        compiler_params=_cparams("parallel", "parallel"),
        name="merge",
    )(h2, *ys, wg, wg, wg, wg, wb)


def _outproj_kernel(m_ref, w_ref, x_ref, nw_ref, *out_refs, last):
    xn = x_ref[...] + jnp.dot(m_ref[...], w_ref[...], preferred_element_type=F32)
    ms = jnp.mean(xn * xn, axis=-1, keepdims=True)
    normed = xn * lax.rsqrt(ms + NORM_EPS) * nw_ref[...]
    if last:
        out_refs[0][...] = normed
    else:
        out_refs[0][...] = xn
        out_refs[1][...] = normed.astype(BF16)


def _outproj(merged, w_out, layer, x2, next_norm_w, last, tm):
    t, d = x2.shape
    tm = min(tm, t)
    row = pl.BlockSpec((tm, d), lambda i: (i, 0))
    if last:
        out_specs = row
        out_shape = jax.ShapeDtypeStruct((t, d), F32)
    else:
        out_specs = (row, row)
        out_shape = (jax.ShapeDtypeStruct((t, d), F32), jax.ShapeDtypeStruct((t, d), BF16))
    return pl.pallas_call(
        functools.partial(_outproj_kernel, last=last),
        grid=(t // tm,),
        in_specs=[row, pl.BlockSpec((None, d, d), lambda i: (layer, 0, 0)), row,
                  pl.BlockSpec((1, d), lambda i: (0, 0))],
        out_specs=out_specs,
        out_shape=out_shape,
        compiler_params=_cparams("parallel"),
        name="outproj",
    )(merged, w_out, x2, next_norm_w.reshape(1, d))


def kernel(x, norm_w, w_in, diff_lambda, diff_subln_w, swa_sinks, ssd_conv_w, ssd_conv_b,
           ssd_dt_bias, ssd_a_log, ssd_d, ssd_norm_w, conf_conv_w, conf_conv_b, conf_ln_w,
           conf_ln_b, w_branch, w_out, rel_bias, final_norm_w):
    b, s, d = x.shape
    depth = w_in.shape[0]
    t = b * s
    tq = min(256, s)
    bias_a = _bias_tiles(rel_bias, tq, 0, A_HEADS, None)
    bias_b = _bias_tiles(rel_bias, WINDOW, A_HEADS, B_HEADS, WINDOW)

    first_scale = np.ones((1, N_FIRST), np.float32)
    first_scale[0, :A_HEADS * 2 * A_QK] = A_QK ** -0.5 * LOG2E
    first_scale[0, COL_BQ:COL_BQ + B_HEADS * B_HD] = B_HD ** -0.5 * LOG2E
    first_scale = jnp.asarray(first_scale)
    ones = lambda n: jnp.ones((1, n), F32)

    w_first = w_in[:, :, :N_FIRST].astype(BF16)
    n_second = SRC_SECOND[1] - SRC_SECOND[0]
    w_rest = _shift_cast(w_in, SRC_SECOND[0], w_in.shape[2] - SRC_SECOND[0], 512)
    w_dt = _dt_weights(w_in)
    w_br = w_branch.astype(BF16)
    w_o = w_out.astype(BF16)

    x2 = x.reshape(t, d)
    h2 = _rmsnorm(x2, norm_w[0], BF16)
    out = None
    for l in range(depth):
        first = _matmul(h2, w_first, l, first_scale, BF16, 1024, 1024, "inproj_first")
        second = _matmul(h2, w_rest, l, ones(n_second), BF16, 1024, 1024, "inproj_second")
        dt_raw = _matmul(h2, w_dt, l, ones(LANES), F32, 1024, LANES, "dtproj")
        p3 = first.reshape(b, s, N_FIRST)
        p3b = second.reshape(b, s, n_second)
        dt3 = dt_raw.reshape(b, s, LANES)

        lam_init = 0.8 - 0.6 * math.exp(-0.3 * l)
        ya = _diff_attention(p3, diff_lambda[l], diff_subln_w[l], rel_bias, bias_a, lam_init, tq)
        yb = _swa(p3, swa_sinks[l], bias_b)
        yc = _ssd(p3, p3b, dt3, ssd_conv_w[l], ssd_conv_b[l], ssd_dt_bias[l], ssd_a_log[l],
                  ssd_d[l], ssd_norm_w[l])
        yd = _conformer(p3b, conf_conv_w[l], conf_conv_b[l], conf_ln_w[l], conf_ln_b[l], 256)

        ys = [y.reshape(t, BRANCH_WIDTH) for y in (ya, yb, yc, yd)]
        merged = _merge(h2, ys, w_rest, n_second, w_br, l, 1024, 256)
        last = l == depth - 1
        next_w = final_norm_w if last else norm_w[l + 1]
        res = _outproj(merged, w_o, l, x2, next_w, last, 512)
        if last:
            out = res
        else:
            x2, h2 = res
    return out.reshape(b, s, d)
```

```python
import functools
import math

import jax
import jax.numpy as jnp
import numpy as np
from jax import lax
from jax.experimental import pallas as pl
from jax.experimental.pallas import tpu as pltpu

F32 = jnp.float32
BF16 = jnp.bfloat16

D_MODEL = 2048
NORM_EPS = 1e-6
BRANCH_WIDTH = 1024
NUM_BUCKETS = 32
MAX_DISTANCE = 128

A_QK = 64
A_V = 128
A_HEADS = 8
B_HD = 64
B_HEADS = 16
B_KV = 4
WINDOW = 128
C_HEADS = 16
C_HD = 64
C_GROUPS = 2
C_STATE = 128
C_CONV = 4
CHUNK = 128
C_XBC = 1536
D_CONV = 31
N_BRANCH = 4

N_FIRST = 8192
COL_BQ = 4096
COL_BK = 5120
COL_BV = 5376
COL_BG = 5632
COL_CXBC = 6656
SRC_DT = (8192, 8208)
SRC_SECOND = (8208, 12304)
COL2_CZ = 0
COL2_DGLU = 1024
COL2_DG = 3072
SRC_MG = 12304

NEG = -1e30
LOG2E = 1.4426950408889634
LANES = 128
VMEM_LIMIT = 56 * 1024 * 1024


def _cparams(*sem):
    return pltpu.CompilerParams(dimension_semantics=sem, vmem_limit_bytes=VMEM_LIMIT)


def _sigmoid(x):
    return 1.0 / (1.0 + jnp.exp(-x))


def _silu(x):
    return x * _sigmoid(x)


def _rmsnorm_kernel(x_ref, w_ref, o_ref):
    x = x_ref[...]
    ms = jnp.mean(x * x, axis=-1, keepdims=True)
    o_ref[...] = (x * lax.rsqrt(ms + NORM_EPS) * w_ref[...]).astype(o_ref.dtype)


def _rmsnorm(x2, w, out_dtype):
    t, d = x2.shape
    tm = min(512, t)
    return pl.pallas_call(
        _rmsnorm_kernel,
        grid=(t // tm,),
        in_specs=[pl.BlockSpec((tm, d), lambda i: (i, 0)),
                  pl.BlockSpec((1, d), lambda i: (0, 0))],
        out_specs=pl.BlockSpec((tm, d), lambda i: (i, 0)),
        out_shape=jax.ShapeDtypeStruct((t, d), out_dtype),
        compiler_params=_cparams("parallel"),
        name="rmsnorm",
    )(x2, w.reshape(1, d))


DN_NT = (((1,), (1,)), ((), ()))
BF16_SUBLANES = 16


def _matmul_kernel(a_ref, bt_ref, cs_ref, o_ref):
    acc = lax.dot_general(a_ref[...], bt_ref[...], DN_NT, preferred_element_type=F32)
    o_ref[...] = (acc * cs_ref[...]).astype(o_ref.dtype)


def _matmul(a, bt, row0, col_scale, out_dtype, tm, tn, name):
    m, k = a.shape
    n = col_scale.shape[1]
    tm = min(tm, m)
    tn = min(tn, n)
    return pl.pallas_call(
        _matmul_kernel,
        grid=(m // tm, n // tn),
        in_specs=[pl.BlockSpec((tm, k), lambda i, j: (i, 0)),
                  pl.BlockSpec((pl.Element(tn), pl.Element(k)),
                               lambda i, j: (pl.multiple_of(row0 + j * tn, BF16_SUBLANES), 0)),
                  pl.BlockSpec((1, tn), lambda i, j: (0, j))],
        out_specs=pl.BlockSpec((tm, tn), lambda i, j: (i, j)),
        out_shape=jax.ShapeDtypeStruct((m, n), out_dtype),
        compiler_params=_cparams("parallel", "parallel"),
        name=name,
    )(a, bt, col_scale)


def _bucket_lower_bounds():
    max_exact = NUM_BUCKETS // 2
    lower = list(range(max_exact))
    for k in range(max_exact, NUM_BUCKETS):
        d = max_exact
        while True:
            large = max_exact + int(math.log(d / max_exact) / math.log(MAX_DISTANCE / max_exact)
                                    * (NUM_BUCKETS - max_exact))
            if min(large, NUM_BUCKETS - 1) >= k:
                break
            d += 1
        lower.append(d)
    return tuple(lower)


BUCKET_LOWER = _bucket_lower_bounds()


def _bias_of_distance(d, tbl_ref, h):
    val = jnp.zeros(d.shape, F32) + tbl_ref[0, h] * LOG2E
    for k in range(1, NUM_BUCKETS):
        val = jnp.where(d >= BUCKET_LOWER[k], tbl_ref[k, h] * LOG2E, val)
    return val


def _bias_tiles_kernel(tbl_ref, o_ref, *, n, head0, window):
    h = pl.program_id(0) + head0
    row = lax.broadcasted_iota(jnp.int32, (n, n), 0)
    col = lax.broadcasted_iota(jnp.int32, (n, n), 1)
    d_prev = row - col + n
    d_cur = row - col
    prev = _bias_of_distance(d_prev, tbl_ref, h)
    if window is not None:
        prev = jnp.where(d_prev < window, prev, NEG)
    o_ref[:, :n] = prev
    o_ref[:, n:] = jnp.where(d_cur >= 0, _bias_of_distance(jnp.maximum(d_cur, 0), tbl_ref, h), NEG)


def _bias_tiles(rel_bias, n, head0, heads, window):
    return pl.pallas_call(
        functools.partial(_bias_tiles_kernel, n=n, head0=head0, window=window),
        grid=(heads,),
        in_specs=[pl.BlockSpec(memory_space=pltpu.SMEM)],
        out_specs=pl.BlockSpec((None, n, 2 * n), lambda h: (h, 0, 0)),
        out_shape=jax.ShapeDtypeStruct((heads, n, 2 * n), F32),
        compiler_params=_cparams("parallel"),
        name="bias_tiles",
    )(rel_bias)


def _lane_fold(x, op):
    r = x[:, :LANES]
    for c in range(1, x.shape[1] // LANES):
        r = op(r, x[:, c * LANES:(c + 1) * LANES])
    return r


def _diffattn_kernel(tbl_ref, lam_ref, q_ref, k_ref, v_ref, g_ref, bias_ref, subw_ref,
                     o_ref, s_ref, *, tq, nq, lam_init):
    h = pl.program_id(1)
    reps = tq // LANES
    dn_nt = (((1,), (1,)), ((), ()))
    cfar = tbl_ref[NUM_BUCKETS - 1, h] * LOG2E
    lp = lam_ref[...]
    lam = (jnp.exp(jnp.sum(lp[0:1] * lp[1:2], axis=1, keepdims=True))
           - jnp.exp(jnp.sum(lp[2:3] * lp[3:4], axis=1, keepdims=True)) + lam_init)
    lane = lax.broadcasted_iota(jnp.int32, (tq, LANES), 1)

    def with_bias(s, b):
        return jnp.concatenate([s[:tq] + b, s[tq:] + b], axis=0)

    for i in range(nq):
        rows = slice(i * tq, (i + 1) * tq)
        q = q_ref[rows, :]
        zero = jnp.zeros_like(q)
        qz = jnp.concatenate([jnp.where(lane < A_QK, q, zero),
                              jnp.where(lane >= A_QK, q, zero)], axis=0)
        slot = i % 2
        mp_far = None
        mp_near = None
        for j in range(i + 1):
            s = lax.dot_general(qz, k_ref[j * tq:(j + 1) * tq, :], dn_nt,
                                preferred_element_type=F32)
            if j == i:
                s = with_bias(s, bias_ref[:, tq:])
            elif j == i - 1:
                s = with_bias(s, bias_ref[:, :tq])
            s_ref[slot, j] = s
            f = _lane_fold(s, jnp.maximum)
            if j >= i - 1:
                mp_near = f if mp_near is None else jnp.maximum(mp_near, f)
            else:
                mp_far = f if mp_far is None else jnp.maximum(mp_far, f)
        mp = mp_near if mp_far is None else jnp.maximum(mp_near, mp_far + cfar)
        m = jnp.broadcast_to(jnp.max(mp, axis=1, keepdims=True), mp.shape)
        m_near = jnp.concatenate([m] * reps, axis=1)
        m_far = jnp.concatenate([m - cfar] * reps, axis=1)
        lpart = None
        acc = None
        for j in range(i + 1):
            p = jnp.exp2(s_ref[slot, j] - (m_near if j >= i - 1 else m_far))
            f = _lane_fold(p, jnp.add)
            pv = jnp.dot(p.astype(BF16), v_ref[j * tq:(j + 1) * tq, :],
                         preferred_element_type=F32)
            lpart = f if lpart is None else lpart + f
            acc = pv if acc is None else acc + pv
        lsum = jnp.broadcast_to(jnp.sum(lpart, axis=1, keepdims=True), lpart.shape)
        o_all = acc / lsum
        o = o_all[:tq] - lam * o_all[tq:]
        ms = jnp.mean(o * o, axis=1, keepdims=True)
        y = o * lax.rsqrt(ms + NORM_EPS) * subw_ref[...] * (1.0 - lam_init)
        o_ref[rows, :] = (y * _silu(g_ref[rows, :].astype(F32))).astype(o_ref.dtype)


def _diff_attention(p3, lam_params, subln_w, rel_bias, bias, lam_init, tq):
    b, s, _ = p3.shape
    nq = s // tq
    assert tq + 1 >= BUCKET_LOWER[-1]

    kernel = functools.partial(_diffattn_kernel, tq=tq, nq=nq, lam_init=lam_init)
    col = lambda off: (lambda bi, h: (bi, 0, off + h))
    return pl.pallas_call(
        kernel,
        grid=(b, A_HEADS),
        in_specs=[
            pl.BlockSpec(memory_space=pltpu.SMEM),
            pl.BlockSpec((4, A_QK), lambda bi, h: (0, 0)),
            pl.BlockSpec((None, s, LANES), col(0)),
            pl.BlockSpec((None, s, LANES), col(A_HEADS)),
            pl.BlockSpec((None, s, LANES), col(2 * A_HEADS)),
            pl.BlockSpec((None, s, LANES), col(3 * A_HEADS)),
            pl.BlockSpec((None, tq, 2 * tq), lambda bi, h: (h, 0, 0)),
            pl.BlockSpec((1, A_V), lambda bi, h: (0, 0)),
        ],
        out_specs=pl.BlockSpec((None, s, LANES), lambda bi, h: (bi, 0, h)),
        out_shape=jax.ShapeDtypeStruct((b, s, BRANCH_WIDTH), BF16),
        scratch_shapes=[pltpu.VMEM((2, nq, 2 * tq, tq), F32)],
        compiler_params=_cparams("parallel", "parallel"),
        name="diff_attention",
    )(rel_bias, lam_params, p3, p3, p3, p3, bias, subln_w.reshape(1, A_V))


def _swa_kernel(sink_ref, q_ref, glo_ref, ghi_ref, kc_ref, kp_ref, vc_ref, vp_ref, bias_ref, o_ref):
    n = pl.program_id(1)
    w = WINDOW
    dn_nt = (((1,), (1,)), ((), ()))
    group = B_HEADS // B_KV
    low = lax.broadcasted_iota(jnp.int32, (w, LANES), 1) < B_HD
    low2 = lax.broadcasted_iota(jnp.int32, (2 * w, LANES), 1) < B_HD
    key_col = lax.broadcasted_iota(jnp.int32, (1, 2 * w), 1)
    pen = jnp.where(key_col < w, jnp.where(n > 0, 0.0, NEG), 0.0).astype(F32)
    kcat = jnp.concatenate([kp_ref[...], kc_ref[...]], axis=0)
    vcat = jnp.concatenate([vp_ref[...], vc_ref[...]], axis=0)

    def dup_halves(x):
        xf = x.astype(F32)
        rolled = pltpu.roll(xf, B_HD, 1)
        return (jnp.where(low2, xf, rolled).astype(BF16),
                jnp.where(low2, rolled, xf).astype(BF16))

    for kvcol in range(B_KV * B_HD // LANES):
        kz = dup_halves(kcat[:, kvcol * LANES:(kvcol + 1) * LANES])
        vz = dup_halves(vcat[:, kvcol * LANES:(kvcol + 1) * LANES])
        for half in range(2):
            kv = 2 * kvcol + half
            qcols = [2 * kv + c for c in range(group // 2)]
            rows = []
            for c in qcols:
                qc = q_ref[:, c * LANES:(c + 1) * LANES]
                zero = jnp.zeros_like(qc)
                rows += [jnp.where(low, qc, zero), jnp.where(low, zero, qc)]
            qz = jnp.concatenate(rows, axis=0)
            s = lax.dot_general(qz, kz[half], dn_nt, preferred_element_type=F32)
            s = s + bias_ref[kv * group:(kv + 1) * group].reshape(group * w, 2 * w) + pen
            sink = jnp.concatenate(
                [jnp.zeros((w, LANES), F32) + sink_ref[kv * group + r] * LOG2E for r in range(group)],
                axis=0)
            m = jnp.max(_lane_fold(s, jnp.maximum), axis=1, keepdims=True)
            m = jnp.maximum(jnp.broadcast_to(m, sink.shape), sink)
            p = jnp.exp2(s - jnp.concatenate([m] * (2 * w // LANES), axis=1))
            l = jnp.sum(_lane_fold(p, jnp.add), axis=1, keepdims=True)
            l = jnp.broadcast_to(l, sink.shape) + jnp.exp2(sink - m)
            o = jnp.dot(p.astype(BF16), vz[half], preferred_element_type=F32) / l
            for ci, c in enumerate(qcols):
                pair = jnp.where(low, o[2 * ci * w:(2 * ci + 1) * w], o[(2 * ci + 1) * w:(2 * ci + 2) * w])
                cols = slice(c * LANES, (c + 1) * LANES)
                half_cols = BRANCH_WIDTH // 2 // LANES
                g_ref = glo_ref if c < half_cols else ghi_ref
                gcols = slice((c % half_cols) * LANES, (c % half_cols + 1) * LANES)
                o_ref[:, cols] = (pair * _silu(g_ref[:, gcols].astype(F32))).astype(o_ref.dtype)


def _swa(p3, sinks, bias):
    b, s, _ = p3.shape
    nb = s // WINDOW

    w = WINDOW
    prev_idx = lambda bi, n, c: (bi, jnp.maximum(n - 1, 0), c)
    return pl.pallas_call(
        _swa_kernel,
        grid=(b, nb),
        in_specs=[
            pl.BlockSpec(memory_space=pltpu.SMEM),
            pl.BlockSpec((None, w, 1024), lambda bi, n: (bi, n, COL_BQ // 1024)),
            pl.BlockSpec((None, w, 512), lambda bi, n: (bi, n, COL_BG // 512)),
            pl.BlockSpec((None, w, 512), lambda bi, n: (bi, n, COL_BG // 512 + 1)),
            pl.BlockSpec((None, w, 256), lambda bi, n: (bi, n, COL_BK // 256)),
            pl.BlockSpec((None, w, 256), lambda bi, n: prev_idx(bi, n, COL_BK // 256)),
            pl.BlockSpec((None, w, 256), lambda bi, n: (bi, n, COL_BV // 256)),
            pl.BlockSpec((None, w, 256), lambda bi, n: prev_idx(bi, n, COL_BV // 256)),
            pl.BlockSpec((B_HEADS, w, 2 * w), lambda bi, n: (0, 0, 0)),
        ],
        out_specs=pl.BlockSpec((None, w, 1024), lambda bi, n: (bi, n, 0)),
        out_shape=jax.ShapeDtypeStruct((b, s, BRANCH_WIDTH), BF16),
        compiler_params=_cparams("parallel", "parallel"),
        name="swa",
    )(sinks, p3, p3, p3, p3, p3, p3, p3, bias)


def _exact_dot01(v, r):
    hi = v.astype(BF16)
    r1 = v - hi.astype(F32)
    mid = r1.astype(BF16)
    lo = (r1 - mid.astype(F32)).astype(BF16)
    return (jnp.dot(hi, r, preferred_element_type=F32)
            + jnp.dot(mid, r, preferred_element_type=F32)
            + jnp.dot(lo, r, preferred_element_type=F32))


def _ssd_kernel(xbc0_ref, xbc1_ref, xbc2_ref, z_ref, dt_ref, cw_ref, cb_ref, dtb_ref, alog_ref,
                dexp_ref, nw_ref, r64_ref, r128_ref, o_ref, ext_ref, state_ref):
    c = pl.program_id(1)
    halo = 8

    @pl.when(c == 0)
    def _():
        ext_ref[0:halo, :] = jnp.zeros((halo, C_XBC), F32)
        state_ref[...] = jnp.zeros(state_ref.shape, F32)

    xraw = jnp.concatenate([xbc0_ref[...], xbc1_ref[...], xbc2_ref[...]], axis=1).astype(F32)
    ext_ref[halo:halo + CHUNK, :] = xraw
    acc = jnp.broadcast_to(cb_ref[...], (CHUNK, C_XBC))
    for j in range(C_CONV):
        off = halo - (C_CONV - 1) + j
        acc = acc + cw_ref[j:j + 1, :] * ext_ref[off:off + CHUNK, :]
    ext_ref[0:halo, :] = xraw[CHUNK - halo:, :]
    xc = _silu(acc)
    xs = xc[:, :1024]
    bm = xc[:, 1024:1024 + C_GROUPS * C_STATE]
    cm = xc[:, 1024 + C_GROUPS * C_STATE:]

    dtx = dt_ref[...] + dtb_ref[...]
    dt = jnp.maximum(dtx, 0.0) + jnp.log1p(jnp.exp(-jnp.abs(dtx)))
    da = dt * (-jnp.exp(alog_ref[...]))
    row = lax.broadcasted_iota(jnp.int32, (CHUNK, CHUNK), 0)
    col = lax.broadcasted_iota(jnp.int32, (CHUNK, CHUNK), 1)
    causal = row >= col
    tri = causal.astype(BF16)
    hi = da.astype(BF16)
    r1 = da - hi.astype(F32)
    mid = r1.astype(BF16)
    lo = (r1 - mid.astype(F32)).astype(BF16)
    a_cum = (jnp.dot(tri, hi, preferred_element_type=F32)
             + jnp.dot(tri, mid, preferred_element_type=F32)
             + jnp.dot(tri, lo, preferred_element_type=F32))
    a_cum_t = a_cum.T

    r64 = r64_ref[...]
    dt_e = _exact_dot01(dt, r64)
    acum_e = _exact_dot01(a_cum, r64)
    acum_col = _exact_dot01(a_cum, r128_ref[...])
    alast_e = acum_e[CHUNK - 1:CHUNK, :]

    xdt = xs * dt_e
    xdt_b = xdt.astype(BF16)
    xw_b = (xdt * jnp.exp(alast_e - acum_e)).astype(BF16)
    chunk_decay = jnp.exp(alast_e)
    bm_b = bm.astype(BF16)
    cm_b = cm.astype(BF16)
    dn_nt = (((1,), (1,)), ((), ()))
    heads_per_group = C_HEADS // C_GROUPS
    gw = heads_per_group * C_HD

    y_parts = []
    for g in range(C_GROUPS):
        bg = bm[:, g * C_STATE:(g + 1) * C_STATE]
        bg_b = bm_b[:, g * C_STATE:(g + 1) * C_STATE]
        cg_b = cm_b[:, g * C_STATE:(g + 1) * C_STATE]
        cb = lax.dot_general(cg_b, bg_b, dn_nt, preferred_element_type=F32)
        bgt_b = bg.T.astype(BF16)
        gcols = slice(g * gw, (g + 1) * gw)
        new_states = jnp.dot(bgt_b, xw_b[:, gcols], preferred_element_type=F32)
        prev = state_ref[g]
        y_off = jnp.dot(cg_b, prev.astype(BF16), preferred_element_type=F32)
        state_ref[g] = prev * chunk_decay[:, gcols] + new_states
        diag = []
        for e in range(heads_per_group):
            hh = g * heads_per_group + e
            seg = acum_col[:, hh * LANES:(hh + 1) * LANES] - a_cum_t[hh:hh + 1, :]
            decay = jnp.exp(jnp.where(causal, seg, NEG))
            mh = (cb * decay).astype(BF16)
            diag.append(jnp.dot(mh, xdt_b[:, hh * C_HD:(hh + 1) * C_HD],
                                preferred_element_type=F32))
        y_diag = jnp.concatenate(diag, axis=1)
        y_g = (y_diag + y_off * jnp.exp(acum_e[:, gcols])
               + xs[:, gcols] * dexp_ref[:, gcols])
        zg = z_ref[:, gcols].astype(F32)
        yg = y_g * _silu(zg)
        ms = jnp.mean(yg * yg, axis=1, keepdims=True)
        y_parts.append(yg * lax.rsqrt(ms + NORM_EPS) * nw_ref[:, gcols])
    o_ref[...] = jnp.concatenate(y_parts, axis=1).astype(o_ref.dtype)


def _ssd(p3, p3b, dt3, conv_w, conv_b, dt_bias, a_log, d_skip, norm_w):
    b, s, _ = p3.shape
    nc = s // CHUNK
    pad = LANES - C_HEADS
    dtb = jnp.pad(dt_bias, (0, pad)).reshape(1, LANES)
    alog = jnp.pad(a_log, (0, pad)).reshape(1, LANES)
    dexp = jnp.repeat(d_skip, C_HD).reshape(1, BRANCH_WIDTH)
    heads = np.arange(LANES)[:, None]
    r64 = jnp.asarray(heads == (np.arange(C_HEADS * C_HD)[None, :] // C_HD), BF16)
    r128 = jnp.asarray(heads == (np.arange(C_HEADS * LANES)[None, :] // LANES), BF16)
    const = lambda bi, c: (0, 0)
    return pl.pallas_call(
        _ssd_kernel,
        grid=(b, nc),
        in_specs=[
            pl.BlockSpec((None, CHUNK, 512), lambda bi, c: (bi, c, COL_CXBC // 512)),
            pl.BlockSpec((None, CHUNK, 512), lambda bi, c: (bi, c, COL_CXBC // 512 + 1)),
            pl.BlockSpec((None, CHUNK, 512), lambda bi, c: (bi, c, COL_CXBC // 512 + 2)),
            pl.BlockSpec((None, CHUNK, 1024), lambda bi, c: (bi, c, COL2_CZ // 1024)),
            pl.BlockSpec((None, CHUNK, LANES), lambda bi, c: (bi, c, 0)),
            pl.BlockSpec((C_CONV, C_XBC), const),
            pl.BlockSpec((1, C_XBC), const),
            pl.BlockSpec((1, LANES), const),
            pl.BlockSpec((1, LANES), const),
            pl.BlockSpec((1, BRANCH_WIDTH), const),
            pl.BlockSpec((1, BRANCH_WIDTH), const),
            pl.BlockSpec((LANES, C_HEADS * C_HD), const),
            pl.BlockSpec((LANES, C_HEADS * LANES), const),
        ],
        out_specs=pl.BlockSpec((None, CHUNK, BRANCH_WIDTH), lambda bi, c: (bi, c, 0)),
        out_shape=jax.ShapeDtypeStruct((b, s, BRANCH_WIDTH), BF16),
        scratch_shapes=[pltpu.VMEM((CHUNK + 8, C_XBC), F32),
                        pltpu.VMEM((C_GROUPS, C_STATE, 512), F32)],
        compiler_params=_cparams("parallel", "arbitrary"),
        name="ssd",
    )(p3, p3, p3, p3b, dt3, conv_w, conv_b.reshape(1, C_XBC), dtb, alog, dexp,
      norm_w.reshape(1, BRANCH_WIDTH), r64, r128)


CONF_HALO = 32


CONF_ROWS_PER_ITER = 32
SUBLANES = 8


def _conformer_kernel(val_ref, gate_ref, g_ref, cw_ref, cb_ref, lnw_ref, lnb_ref, o_ref,
                      ext_ref, sh_ref, y_ref, *, ts):
    t = pl.program_id(1)

    @pl.when(t == 0)
    def _():
        ext_ref[0:CONF_HALO, :] = jnp.zeros((CONF_HALO, BRANCH_WIDTH), F32)

    ext_ref[CONF_HALO:CONF_HALO + ts, :] = (val_ref[...].astype(F32)
                                            * _sigmoid(gate_ref[...].astype(F32)))
    span = ts + CONF_HALO - SUBLANES
    for r in range(1, SUBLANES):
        sh_ref[r - 1] = ext_ref[r:r + span, :]

    first = CONF_HALO - (D_CONV - 1)
    for c in range(BRANCH_WIDTH // LANES):
        cols = slice(c * LANES, (c + 1) * LANES)
        taps = [jnp.broadcast_to(cw_ref[j:j + 1, cols], (SUBLANES, LANES)) for j in range(D_CONV)]
        bias = jnp.broadcast_to(cb_ref[:, cols], (SUBLANES, LANES))

        def body(it, carry):
            base = pl.multiple_of(it * CONF_ROWS_PER_ITER, CONF_ROWS_PER_ITER)
            for u in range(CONF_ROWS_PER_ITER // SUBLANES):
                parts = [bias, None]
                for j in range(D_CONV):
                    r = (first + j) % SUBLANES
                    row = base + u * SUBLANES + (first + j) - r
                    src = ext_ref if r == 0 else sh_ref.at[r - 1]
                    term = taps[j] * src[pl.ds(row, SUBLANES), cols]
                    parts[j % 2] = term if parts[j % 2] is None else parts[j % 2] + term
                y_ref[pl.ds(base + u * SUBLANES, SUBLANES), cols] = parts[0] + parts[1]
            return carry

        lax.fori_loop(0, ts // CONF_ROWS_PER_ITER, body, 0)

    ext_ref[0:CONF_HALO, :] = ext_ref[ts:ts + CONF_HALO, :]
    acc = y_ref[...]
    mu = jnp.mean(acc, axis=1, keepdims=True)
    xc = acc - mu
    var = jnp.mean(xc * xc, axis=1, keepdims=True)
    y = xc * lax.rsqrt(var + NORM_EPS) * lnw_ref[...] + lnb_ref[...]
    o_ref[...] = (_silu(y) * _silu(g_ref[...].astype(F32))).astype(o_ref.dtype)


def _conformer(p3b, conv_w, conv_b, ln_w, ln_b, ts):
    b, s, _ = p3b.shape
    ts = min(ts, s)
    const = lambda bi, t: (0, 0)
    w = BRANCH_WIDTH
    return pl.pallas_call(
        functools.partial(_conformer_kernel, ts=ts),
        grid=(b, s // ts),
        in_specs=[
            pl.BlockSpec((None, ts, w), lambda bi, t: (bi, t, COL2_DGLU // w)),
            pl.BlockSpec((None, ts, w), lambda bi, t: (bi, t, COL2_DGLU // w + 1)),
            pl.BlockSpec((None, ts, w), lambda bi, t: (bi, t, COL2_DG // w)),
            pl.BlockSpec((D_CONV, w), const),
            pl.BlockSpec((1, w), const),
            pl.BlockSpec((1, w), const),
            pl.BlockSpec((1, w), const),
        ],
        out_specs=pl.BlockSpec((None, ts, w), lambda bi, t: (bi, t, 0)),
        out_shape=jax.ShapeDtypeStruct((b, s, w), BF16),
        scratch_shapes=[pltpu.VMEM((CONF_HALO + ts, w), F32),
                        pltpu.VMEM((SUBLANES - 1, CONF_HALO + ts - SUBLANES, w), F32),
                        pltpu.VMEM((ts, w), F32)],
        compiler_params=_cparams("parallel", "arbitrary"),
        name="conformer",
    )(p3b, p3b, p3b, conv_w, conv_b.reshape(1, w), ln_w.reshape(1, w), ln_b.reshape(1, w))


def _merge_kernel(h_ref, ya_ref, yb_ref, yc_ref, yd_ref, wg0, wg1, wg2, wg3, wb_ref, o_ref):
    h = h_ref[...]
    acc = None
    for i, (y_ref, wg_ref) in enumerate(((ya_ref, wg0), (yb_ref, wg1), (yc_ref, wg2), (yd_ref, wg3))):
        gate = lax.dot_general(h, wg_ref[...], DN_NT, preferred_element_type=F32)
        up = jnp.dot(y_ref[...], wb_ref[i], preferred_element_type=F32)
        term = _sigmoid(gate) * up
        acc = term if acc is None else acc + term
    o_ref[...] = acc.astype(o_ref.dtype)


def _merge(h2, ys, wgt, row0, wb, layer, tm, tn):
    t, d = h2.shape
    tm = min(tm, t)
    nblk = d // tn
    y_spec = pl.BlockSpec((tm, BRANCH_WIDTH), lambda i, j: (i, 0))
    wg_specs = [pl.BlockSpec((pl.Element(tn), pl.Element(d)),
                             functools.partial(
                                 lambda i, j, k: (pl.multiple_of(row0 + (k * nblk + j) * tn,
                                                                 BF16_SUBLANES), 0), k=k))
                for k in range(N_BRANCH)]
    return pl.pallas_call(
        _merge_kernel,
        grid=(t // tm, nblk),
        in_specs=[pl.BlockSpec((tm, d), lambda i, j: (i, 0)), y_spec, y_spec, y_spec, y_spec,
                  *wg_specs,
                  pl.BlockSpec((None, N_BRANCH, BRANCH_WIDTH, tn), lambda i, j: (layer, 0, 0, j))],
        out_specs=pl.BlockSpec((tm, tn), lambda i, j: (i, j)),
        out_shape=jax.ShapeDtypeStruct((t, d), BF16),
        compiler_params=_cparams("parallel", "parallel"),
        name="merge",
    )(h2, *ys, wgt, wgt, wgt, wgt, wb)


def _outproj_kernel(m_ref, w_ref, x_ref, nw_ref, *out_refs, last):
    xn = x_ref[...] + jnp.dot(m_ref[...], w_ref[...], preferred_element_type=F32)
    ms = jnp.mean(xn * xn, axis=-1, keepdims=True)
    normed = xn * lax.rsqrt(ms + NORM_EPS) * nw_ref[...]
    if last:
        out_refs[0][...] = normed
    else:
        out_refs[0][...] = xn
        out_refs[1][...] = normed.astype(BF16)


def _outproj(merged, w_out, layer, x2, next_norm_w, last, tm):
    t, d = x2.shape
    tm = min(tm, t)
    row = pl.BlockSpec((tm, d), lambda i: (i, 0))
    if last:
        out_specs = row
        out_shape = jax.ShapeDtypeStruct((t, d), F32)
    else:
        out_specs = (row, row)
        out_shape = (jax.ShapeDtypeStruct((t, d), F32), jax.ShapeDtypeStruct((t, d), BF16))
    return pl.pallas_call(
        functools.partial(_outproj_kernel, last=last),
        grid=(t // tm,),
        in_specs=[row, pl.BlockSpec((None, d, d), lambda i: (layer, 0, 0)), row,
                  pl.BlockSpec((1, d), lambda i: (0, 0))],
        out_specs=out_specs,
        out_shape=out_shape,
        compiler_params=_cparams("parallel"),
        name="outproj",
    )(merged, w_out, x2, next_norm_w.reshape(1, d))


def kernel(x, norm_w, w_in, diff_lambda, diff_subln_w, swa_sinks, ssd_conv_w, ssd_conv_b,
           ssd_dt_bias, ssd_a_log, ssd_d, ssd_norm_w, conf_conv_w, conf_conv_b, conf_ln_w,
           conf_ln_b, w_branch, w_out, rel_bias, final_norm_w):
    b, s, d = x.shape
    depth = w_in.shape[0]
    t = b * s
    tq = min(256, s)
    bias_a = _bias_tiles(rel_bias, tq, 0, A_HEADS, None)
    bias_b = _bias_tiles(rel_bias, WINDOW, A_HEADS, B_HEADS, WINDOW)

    first_scale = np.ones((1, N_FIRST), np.float32)
    first_scale[0, :A_HEADS * 2 * A_QK] = A_QK ** -0.5 * LOG2E
    first_scale[0, COL_BQ:COL_BQ + B_HEADS * B_HD] = B_HD ** -0.5 * LOG2E
    first_scale = jnp.asarray(first_scale)
    ones = lambda n: jnp.ones((1, n), F32)

    n_in = w_in.shape[2]
    w_t = jnp.swapaxes(w_in, 1, 2).astype(BF16).reshape(depth * n_in, d)
    n_second = SRC_SECOND[1] - SRC_SECOND[0]
    w_br = w_branch.astype(BF16)
    w_o = w_out.astype(BF16)

    x2 = x.reshape(t, d)
    h2 = _rmsnorm(x2, norm_w[0], BF16)
    out = None
    for l in range(depth):
        first = _matmul(h2, w_t, l * n_in, first_scale, BF16, 1024, 1024, "inproj_first")
        second = _matmul(h2, w_t, l * n_in + SRC_SECOND[0], ones(n_second), BF16, 1024, 1024,
                         "inproj_second")
        dt_raw = _matmul(h2, w_t, l * n_in + SRC_DT[0], ones(LANES), F32, 1024, LANES, "dtproj")
        p3 = first.reshape(b, s, N_FIRST)
        p3b = second.reshape(b, s, n_second)
        dt3 = dt_raw.reshape(b, s, LANES)

        lam_init = 0.8 - 0.6 * math.exp(-0.3 * l)
        ya = _diff_attention(p3, diff_lambda[l], diff_subln_w[l], rel_bias, bias_a, lam_init, tq)
        yb = _swa(p3, swa_sinks[l], bias_b)
        yc = _ssd(p3, p3b, dt3, ssd_conv_w[l], ssd_conv_b[l], ssd_dt_bias[l], ssd_a_log[l],
                  ssd_d[l], ssd_norm_w[l])
        yd = _conformer(p3b, conf_conv_w[l], conf_conv_b[l], conf_ln_w[l], conf_ln_b[l], 256)

        ys = [y.reshape(t, BRANCH_WIDTH) for y in (ya, yb, yc, yd)]
        merged = _merge(h2, ys, w_t, l * n_in + SRC_MG, w_br, l, 1024, 256)
        last = l == depth - 1
        next_w = final_norm_w if last else norm_w[l + 1]
        res = _outproj(merged, w_o, l, x2, next_w, last, 512)
        if last:
            out = res
        else:
            x2, h2 = res
    return out.reshape(b, s, d)
```

```python
import functools
import math

import jax
import jax.numpy as jnp
import numpy as np
from jax import lax
from jax.experimental import pallas as pl
from jax.experimental.pallas import tpu as pltpu

F32 = jnp.float32
BF16 = jnp.bfloat16

D_MODEL = 2048
NORM_EPS = 1e-6
BRANCH_WIDTH = 1024
NUM_BUCKETS = 32
MAX_DISTANCE = 128

A_QK = 64
A_V = 128
A_HEADS = 8
B_HD = 64
B_HEADS = 16
B_KV = 4
WINDOW = 128
C_HEADS = 16
C_HD = 64
C_GROUPS = 2
C_STATE = 128
C_CONV = 4
CHUNK = 128
C_XBC = 1536
D_CONV = 31
N_BRANCH = 4

N_FIRST = 8192
COL_BQ = 4096
COL_BK = 5120
COL_BV = 5376
COL_BG = 5632
COL_CXBC = 6656
SRC_DT = (8192, 8208)
SRC_SECOND = (8208, 12304)
COL2_CZ = 0
COL2_DGLU = 1024
COL2_DG = 3072
SRC_MG = 12304

NEG = -1e30
LOG2E = 1.4426950408889634
LANES = 128
VMEM_LIMIT = 56 * 1024 * 1024


def _cparams(*sem):
    return pltpu.CompilerParams(dimension_semantics=sem, vmem_limit_bytes=VMEM_LIMIT)


def _sigmoid(x):
    return 1.0 / (1.0 + jnp.exp(-x))


def _silu(x):
    return x * _sigmoid(x)


def _rmsnorm_kernel(x_ref, w_ref, o_ref):
    x = x_ref[...]
    ms = jnp.mean(x * x, axis=-1, keepdims=True)
    o_ref[...] = (x * lax.rsqrt(ms + NORM_EPS) * w_ref[...]).astype(o_ref.dtype)


def _rmsnorm(x2, w, out_dtype):
    t, d = x2.shape
    tm = min(512, t)
    return pl.pallas_call(
        _rmsnorm_kernel,
        grid=(t // tm,),
        in_specs=[pl.BlockSpec((tm, d), lambda i: (i, 0)),
                  pl.BlockSpec((1, d), lambda i: (0, 0))],
        out_specs=pl.BlockSpec((tm, d), lambda i: (i, 0)),
        out_shape=jax.ShapeDtypeStruct((t, d), out_dtype),
        compiler_params=_cparams("parallel"),
        name="rmsnorm",
    )(x2, w.reshape(1, d))


DN_NT = (((1,), (1,)), ((), ()))
BF16_SUBLANES = 16


def _matmul_kernel(a_ref, bt_ref, cs_ref, o_ref):
    acc = lax.dot_general(a_ref[...], bt_ref[...], DN_NT, preferred_element_type=F32)
    o_ref[...] = (acc * cs_ref[...]).astype(o_ref.dtype)


def _matmul(a, bt, row0, col_scale, out_dtype, tm, tn, name):
    m, k = a.shape
    n = col_scale.shape[1]
    tm = min(tm, m)
    tn = min(tn, n)
    return pl.pallas_call(
        _matmul_kernel,
        grid=(m // tm, n // tn),
        in_specs=[pl.BlockSpec((tm, k), lambda i, j: (i, 0)),
                  pl.BlockSpec((pl.Element(tn), pl.Element(k)),
                               lambda i, j: (pl.multiple_of(row0 + j * tn, BF16_SUBLANES), 0)),
                  pl.BlockSpec((1, tn), lambda i, j: (0, j))],
        out_specs=pl.BlockSpec((tm, tn), lambda i, j: (i, j)),
        out_shape=jax.ShapeDtypeStruct((m, n), out_dtype),
        compiler_params=_cparams("parallel", "parallel"),
        name=name,
    )(a, bt, col_scale)


def _bucket_lower_bounds():
    max_exact = NUM_BUCKETS // 2
    lower = list(range(max_exact))
    for k in range(max_exact, NUM_BUCKETS):
        d = max_exact
        while True:
            large = max_exact + int(math.log(d / max_exact) / math.log(MAX_DISTANCE / max_exact)
                                    * (NUM_BUCKETS - max_exact))
            if min(large, NUM_BUCKETS - 1) >= k:
                break
            d += 1
        lower.append(d)
    return tuple(lower)


BUCKET_LOWER = _bucket_lower_bounds()


def _bias_of_distance(d, tbl_ref, h):
    val = jnp.zeros(d.shape, F32) + tbl_ref[0, h] * LOG2E
    for k in range(1, NUM_BUCKETS):
        val = jnp.where(d >= BUCKET_LOWER[k], tbl_ref[k, h] * LOG2E, val)
    return val


def _bias_tiles_kernel(tbl_ref, o_ref, *, n, head0, window):
    h = pl.program_id(0) + head0
    row = lax.broadcasted_iota(jnp.int32, (n, n), 0)
    col = lax.broadcasted_iota(jnp.int32, (n, n), 1)
    d_prev = row - col + n
    d_cur = row - col
    prev = _bias_of_distance(d_prev, tbl_ref, h)
    if window is not None:
        prev = jnp.where(d_prev < window, prev, NEG)
    o_ref[:, :n] = prev
    o_ref[:, n:] = jnp.where(d_cur >= 0, _bias_of_distance(jnp.maximum(d_cur, 0), tbl_ref, h), NEG)


def _bias_tiles(rel_bias, n, head0, heads, window):
    return pl.pallas_call(
        functools.partial(_bias_tiles_kernel, n=n, head0=head0, window=window),
        grid=(heads,),
        in_specs=[pl.BlockSpec(memory_space=pltpu.SMEM)],
        out_specs=pl.BlockSpec((None, n, 2 * n), lambda h: (h, 0, 0)),
        out_shape=jax.ShapeDtypeStruct((heads, n, 2 * n), F32),
        compiler_params=_cparams("parallel"),
        name="bias_tiles",
    )(rel_bias)


def _lane_fold(x, op):
    r = x[:, :LANES]
    for c in range(1, x.shape[1] // LANES):
        r = op(r, x[:, c * LANES:(c + 1) * LANES])
    return r


def _diffattn_kernel(tbl_ref, lam_ref, q_ref, k_ref, v_ref, g_ref, bias_ref, subw_ref,
                     o_ref, s_ref, p_ref, vext_ref, *, tq, nq, lam_init):
    h = pl.program_id(1)
    reps = tq // LANES
    dn_nt = (((1,), (1,)), ((), ()))
    cfar = tbl_ref[NUM_BUCKETS - 1, h] * LOG2E
    lp = lam_ref[...]
    lam = (jnp.exp(jnp.sum(lp[0:1] * lp[1:2], axis=1, keepdims=True))
           - jnp.exp(jnp.sum(lp[2:3] * lp[3:4], axis=1, keepdims=True)) + lam_init)
    lane = lax.broadcasted_iota(jnp.int32, (tq, LANES), 1)
    vext_ref[:, :A_V] = v_ref[...]
    vext_ref[:, A_V:] = jnp.ones((nq * tq, LANES), BF16)

    def with_bias(s, b):
        return jnp.concatenate([s[:tq] + b, s[tq:] + b], axis=0)

    for i in reversed(range(nq)):
        rows = slice(i * tq, (i + 1) * tq)
        q = q_ref[rows, :]
        zero = jnp.zeros_like(q)
        qz = jnp.concatenate([jnp.where(lane < A_QK, q, zero),
                              jnp.where(lane >= A_QK, q, zero)], axis=0)
        slot = i % 2
        mp_far = None
        mp_near = None
        for j in range(i + 1):
            s = lax.dot_general(qz, k_ref[j * tq:(j + 1) * tq, :], dn_nt,
                                preferred_element_type=F32)
            if j == i:
                s = with_bias(s, bias_ref[:, tq:])
            elif j == i - 1:
                s = with_bias(s, bias_ref[:, :tq])
            s_ref[slot, j] = s
            f = _lane_fold(s, jnp.maximum)
            if j >= i - 1:
                mp_near = f if mp_near is None else jnp.maximum(mp_near, f)
            else:
                mp_far = f if mp_far is None else jnp.maximum(mp_far, f)
        mp = mp_near if mp_far is None else jnp.maximum(mp_near, mp_far + cfar)
        m = jnp.broadcast_to(jnp.max(mp, axis=1, keepdims=True), mp.shape)
        m_near = jnp.concatenate([m] * reps, axis=1)
        m_far = jnp.concatenate([m - cfar] * reps, axis=1)
        for j in range(i + 1):
            p = jnp.exp2(s_ref[slot, j] - (m_near if j >= i - 1 else m_far))
            p_ref[slot, :, j * tq:(j + 1) * tq] = p.astype(BF16)
        nk = (i + 1) * tq
        res = jnp.dot(p_ref[slot, :, :nk], vext_ref[:nk, :], preferred_element_type=F32)
        o_all = res[:, :A_V] / res[:, A_V:]
        o = o_all[:tq] - lam * o_all[tq:]
        ms = jnp.mean(o * o, axis=1, keepdims=True)
        y = o * lax.rsqrt(ms + NORM_EPS) * subw_ref[...] * (1.0 - lam_init)
        o_ref[rows, :] = (y * _silu(g_ref[rows, :].astype(F32))).astype(o_ref.dtype)


def _diff_attention(p3, lam_params, subln_w, rel_bias, bias, lam_init, tq):
    b, s, _ = p3.shape
    nq = s // tq
    assert tq + 1 >= BUCKET_LOWER[-1]

    kernel = functools.partial(_diffattn_kernel, tq=tq, nq=nq, lam_init=lam_init)
    col = lambda off: (lambda bi, h: (bi, 0, off + h))
    return pl.pallas_call(
        kernel,
        grid=(b, A_HEADS),
        in_specs=[
            pl.BlockSpec(memory_space=pltpu.SMEM),
            pl.BlockSpec((4, A_QK), lambda bi, h: (0, 0)),
            pl.BlockSpec((None, s, LANES), col(0)),
            pl.BlockSpec((None, s, LANES), col(A_HEADS)),
            pl.BlockSpec((None, s, LANES), col(2 * A_HEADS)),
            pl.BlockSpec((None, s, LANES), col(3 * A_HEADS)),
            pl.BlockSpec((None, tq, 2 * tq), lambda bi, h: (h, 0, 0)),
            pl.BlockSpec((1, A_V), lambda bi, h: (0, 0)),
        ],
        out_specs=pl.BlockSpec((None, s, LANES), lambda bi, h: (bi, 0, h)),
        out_shape=jax.ShapeDtypeStruct((b, s, BRANCH_WIDTH), BF16),
        scratch_shapes=[pltpu.VMEM((2, nq, 2 * tq, tq), F32),
                        pltpu.VMEM((2, 2 * tq, s), BF16),
                        pltpu.VMEM((s, A_V + LANES), BF16)],
        compiler_params=_cparams("parallel", "parallel"),
        name="diff_attention",
    )(rel_bias, lam_params, p3, p3, p3, p3, bias, subln_w.reshape(1, A_V))


def _swa_kernel(sink_ref, q_ref, glo_ref, ghi_ref, kc_ref, kp_ref, vc_ref, vp_ref, bias_ref, o_ref):
    n = pl.program_id(1)
    w = WINDOW
    dn_nt = (((1,), (1,)), ((), ()))
    group = B_HEADS // B_KV
    low = lax.broadcasted_iota(jnp.int32, (w, LANES), 1) < B_HD
    low2 = lax.broadcasted_iota(jnp.int32, (2 * w, LANES), 1) < B_HD
    key_col = lax.broadcasted_iota(jnp.int32, (1, 2 * w), 1)
    pen = jnp.where(key_col < w, jnp.where(n > 0, 0.0, NEG), 0.0).astype(F32)
    ones_cols = jnp.ones((2 * w, LANES), BF16)
    kcat = jnp.concatenate([kp_ref[...], kc_ref[...]], axis=0)
    vcat = jnp.concatenate([vp_ref[...], vc_ref[...]], axis=0)

    def dup_halves(x):
        xf = x.astype(F32)
        rolled = pltpu.roll(xf, B_HD, 1)
        return (jnp.where(low2, xf, rolled).astype(BF16),
                jnp.where(low2, rolled, xf).astype(BF16))

    for kvcol in range(B_KV * B_HD // LANES):
        kz = dup_halves(kcat[:, kvcol * LANES:(kvcol + 1) * LANES])
        vz = dup_halves(vcat[:, kvcol * LANES:(kvcol + 1) * LANES])
        for half in range(2):
            kv = 2 * kvcol + half
            qcols = [2 * kv + c for c in range(group // 2)]
            rows = []
            for c in qcols:
                qc = q_ref[:, c * LANES:(c + 1) * LANES]
                zero = jnp.zeros_like(qc)
                rows += [jnp.where(low, qc, zero), jnp.where(low, zero, qc)]
            qz = jnp.concatenate(rows, axis=0)
            s = lax.dot_general(qz, kz[half], dn_nt, preferred_element_type=F32)
            s = s + bias_ref[kv * group:(kv + 1) * group].reshape(group * w, 2 * w) + pen
            sink = jnp.concatenate(
                [jnp.zeros((w, LANES), F32) + sink_ref[kv * group + r] * LOG2E for r in range(group)],
                axis=0)
            m = jnp.max(_lane_fold(s, jnp.maximum), axis=1, keepdims=True)
            m = jnp.maximum(jnp.broadcast_to(m, sink.shape), sink)
            p = jnp.exp2(s - jnp.concatenate([m] * (2 * w // LANES), axis=1))
            res = jnp.dot(p.astype(BF16), jnp.concatenate([vz[half], ones_cols], axis=1),
                          preferred_element_type=F32)
            o = res[:, :LANES] / (res[:, LANES:] + jnp.exp2(sink - m))
            for ci, c in enumerate(qcols):
                pair = jnp.where(low, o[2 * ci * w:(2 * ci + 1) * w], o[(2 * ci + 1) * w:(2 * ci + 2) * w])
                cols = slice(c * LANES, (c + 1) * LANES)
                half_cols = BRANCH_WIDTH // 2 // LANES
                g_ref = glo_ref if c < half_cols else ghi_ref
                gcols = slice((c % half_cols) * LANES, (c % half_cols + 1) * LANES)
                o_ref[:, cols] = (pair * _silu(g_ref[:, gcols].astype(F32))).astype(o_ref.dtype)


def _swa(p3, sinks, bias):
    b, s, _ = p3.shape
    nb = s // WINDOW

    w = WINDOW
    prev_idx = lambda bi, n, c: (bi, jnp.maximum(n - 1, 0), c)
    return pl.pallas_call(
        _swa_kernel,
        grid=(b, nb),
        in_specs=[
            pl.BlockSpec(memory_space=pltpu.SMEM),
            pl.BlockSpec((None, w, 1024), lambda bi, n: (bi, n, COL_BQ // 1024)),
            pl.BlockSpec((None, w, 512), lambda bi, n: (bi, n, COL_BG // 512)),
            pl.BlockSpec((None, w, 512), lambda bi, n: (bi, n, COL_BG // 512 + 1)),
            pl.BlockSpec((None, w, 256), lambda bi, n: (bi, n, COL_BK // 256)),
            pl.BlockSpec((None, w, 256), lambda bi, n: prev_idx(bi, n, COL_BK // 256)),
            pl.BlockSpec((None, w, 256), lambda bi, n: (bi, n, COL_BV // 256)),
            pl.BlockSpec((None, w, 256), lambda bi, n: prev_idx(bi, n, COL_BV // 256)),
            pl.BlockSpec((B_HEADS, w, 2 * w), lambda bi, n: (0, 0, 0)),
        ],
        out_specs=pl.BlockSpec((None, w, 1024), lambda bi, n: (bi, n, 0)),
        out_shape=jax.ShapeDtypeStruct((b, s, BRANCH_WIDTH), BF16),
        compiler_params=_cparams("parallel", "parallel"),
        name="swa",
    )(sinks, p3, p3, p3, p3, p3, p3, p3, bias)


def _exact_dot01(v, r):
    hi = v.astype(BF16)
    r1 = v - hi.astype(F32)
    mid = r1.astype(BF16)
    lo = (r1 - mid.astype(F32)).astype(BF16)
    return (jnp.dot(hi, r, preferred_element_type=F32)
            + jnp.dot(mid, r, preferred_element_type=F32)
            + jnp.dot(lo, r, preferred_element_type=F32))


def _ssd_kernel(xbc0_ref, xbc1_ref, xbc2_ref, z_ref, dt_ref, cw_ref, cb_ref, dtb_ref, alog_ref,
                dexp_ref, nw_ref, r64_ref, r128_ref, o_ref, ext_ref, state_ref):
    c = pl.program_id(1)
    halo = 8

    @pl.when(c == 0)
    def _():
        ext_ref[0:halo, :] = jnp.zeros((halo, C_XBC), F32)
        state_ref[...] = jnp.zeros(state_ref.shape, F32)

    xraw = jnp.concatenate([xbc0_ref[...], xbc1_ref[...], xbc2_ref[...]], axis=1).astype(F32)
    ext_ref[halo:halo + CHUNK, :] = xraw
    acc = jnp.broadcast_to(cb_ref[...], (CHUNK, C_XBC))
    for j in range(C_CONV):
        off = halo - (C_CONV - 1) + j
        acc = acc + cw_ref[j:j + 1, :] * ext_ref[off:off + CHUNK, :]
    ext_ref[0:halo, :] = xraw[CHUNK - halo:, :]
    xc = _silu(acc)
    xs = xc[:, :1024]
    bm = xc[:, 1024:1024 + C_GROUPS * C_STATE]
    cm = xc[:, 1024 + C_GROUPS * C_STATE:]

    dtx = dt_ref[...] + dtb_ref[...]
    dt = jnp.maximum(dtx, 0.0) + jnp.log1p(jnp.exp(-jnp.abs(dtx)))
    da = dt * (-jnp.exp(alog_ref[...]))
    row = lax.broadcasted_iota(jnp.int32, (CHUNK, CHUNK), 0)
    col = lax.broadcasted_iota(jnp.int32, (CHUNK, CHUNK), 1)
    causal = row >= col
    tri = causal.astype(BF16)
    hi = da.astype(BF16)
    r1 = da - hi.astype(F32)
    mid = r1.astype(BF16)
    lo = (r1 - mid.astype(F32)).astype(BF16)
    a_cum = (jnp.dot(tri, hi, preferred_element_type=F32)
             + jnp.dot(tri, mid, preferred_element_type=F32)
             + jnp.dot(tri, lo, preferred_element_type=F32))
    a_cum_t = a_cum.T

    r64 = r64_ref[...]
    dt_e = _exact_dot01(dt, r64)
    acum_e = _exact_dot01(a_cum, r64)
    acum_col = _exact_dot01(a_cum, r128_ref[...])
    alast_e = acum_e[CHUNK - 1:CHUNK, :]

    xdt = xs * dt_e
    xdt_b = xdt.astype(BF16)
    xw_b = (xdt * jnp.exp(alast_e - acum_e)).astype(BF16)
    chunk_decay = jnp.exp(alast_e)
    bm_b = bm.astype(BF16)
    cm_b = cm.astype(BF16)
    dn_nt = (((1,), (1,)), ((), ()))
    heads_per_group = C_HEADS // C_GROUPS
    gw = heads_per_group * C_HD

    y_parts = []
    for g in range(C_GROUPS):
        bg = bm[:, g * C_STATE:(g + 1) * C_STATE]
        bg_b = bm_b[:, g * C_STATE:(g + 1) * C_STATE]
        cg_b = cm_b[:, g * C_STATE:(g + 1) * C_STATE]
        cb = lax.dot_general(cg_b, bg_b, dn_nt, preferred_element_type=F32)
        bgt_b = bg.T.astype(BF16)
        gcols = slice(g * gw, (g + 1) * gw)
        new_states = jnp.dot(bgt_b, xw_b[:, gcols], preferred_element_type=F32)
        prev = state_ref[g]
        y_off = jnp.dot(cg_b, prev.astype(BF16), preferred_element_type=F32)
        state_ref[g] = prev * chunk_decay[:, gcols] + new_states
        diag = []
        for e in range(heads_per_group):
            hh = g * heads_per_group + e
            seg = acum_col[:, hh * LANES:(hh + 1) * LANES] - a_cum_t[hh:hh + 1, :]
            decay = jnp.exp(jnp.where(causal, seg, NEG))
            mh = (cb * decay).astype(BF16)
            diag.append(jnp.dot(mh, xdt_b[:, hh * C_HD:(hh + 1) * C_HD],
                                preferred_element_type=F32))
        y_diag = jnp.concatenate(diag, axis=1)
        y_g = (y_diag + y_off * jnp.exp(acum_e[:, gcols])
               + xs[:, gcols] * dexp_ref[:, gcols])
        zg = z_ref[:, gcols].astype(F32)
        yg = y_g * _silu(zg)
        ms = jnp.mean(yg * yg, axis=1, keepdims=True)
        y_parts.append(yg * lax.rsqrt(ms + NORM_EPS) * nw_ref[:, gcols])
    o_ref[...] = jnp.concatenate(y_parts, axis=1).astype(o_ref.dtype)


def _ssd(p3, p3b, dt3, conv_w, conv_b, dt_bias, a_log, d_skip, norm_w):
    b, s, _ = p3.shape
    nc = s // CHUNK
    pad = LANES - C_HEADS
    dtb = jnp.pad(dt_bias, (0, pad)).reshape(1, LANES)
    alog = jnp.pad(a_log, (0, pad)).reshape(1, LANES)
    dexp = jnp.repeat(d_skip, C_HD).reshape(1, BRANCH_WIDTH)
    heads = np.arange(LANES)[:, None]
    r64 = jnp.asarray(heads == (np.arange(C_HEADS * C_HD)[None, :] // C_HD), BF16)
    r128 = jnp.asarray(heads == (np.arange(C_HEADS * LANES)[None, :] // LANES), BF16)
    const = lambda bi, c: (0, 0)
    return pl.pallas_call(
        _ssd_kernel,
        grid=(b, nc),
        in_specs=[
            pl.BlockSpec((None, CHUNK, 512), lambda bi, c: (bi, c, COL_CXBC // 512)),
            pl.BlockSpec((None, CHUNK, 512), lambda bi, c: (bi, c, COL_CXBC // 512 + 1)),
            pl.BlockSpec((None, CHUNK, 512), lambda bi, c: (bi, c, COL_CXBC // 512 + 2)),
            pl.BlockSpec((None, CHUNK, 1024), lambda bi, c: (bi, c, COL2_CZ // 1024)),
            pl.BlockSpec((None, CHUNK, LANES), lambda bi, c: (bi, c, 0)),
            pl.BlockSpec((C_CONV, C_XBC), const),
            pl.BlockSpec((1, C_XBC), const),
            pl.BlockSpec((1, LANES), const),
            pl.BlockSpec((1, LANES), const),
            pl.BlockSpec((1, BRANCH_WIDTH), const),
            pl.BlockSpec((1, BRANCH_WIDTH), const),
            pl.BlockSpec((LANES, C_HEADS * C_HD), const),
            pl.BlockSpec((LANES, C_HEADS * LANES), const),
        ],
        out_specs=pl.BlockSpec((None, CHUNK, BRANCH_WIDTH), lambda bi, c: (bi, c, 0)),
        out_shape=jax.ShapeDtypeStruct((b, s, BRANCH_WIDTH), BF16),
        scratch_shapes=[pltpu.VMEM((CHUNK + 8, C_XBC), F32),
                        pltpu.VMEM((C_GROUPS, C_STATE, 512), F32)],
        compiler_params=_cparams("parallel", "arbitrary"),
        name="ssd",
    )(p3, p3, p3, p3b, dt3, conv_w, conv_b.reshape(1, C_XBC), dtb, alog, dexp,
      norm_w.reshape(1, BRANCH_WIDTH), r64, r128)


CONF_HALO = 32


CONF_ROWS_PER_ITER = 32
SUBLANES = 8


def _conformer_kernel(val_ref, gate_ref, g_ref, cw_ref, cb_ref, lnw_ref, lnb_ref, o_ref,
                      ext_ref, sh_ref, y_ref, *, ts):
    t = pl.program_id(1)

    @pl.when(t == 0)
    def _():
        ext_ref[0:CONF_HALO, :] = jnp.zeros((CONF_HALO, BRANCH_WIDTH), F32)

    ext_ref[CONF_HALO:CONF_HALO + ts, :] = (val_ref[...].astype(F32)
                                            * _sigmoid(gate_ref[...].astype(F32)))
    span = ts + CONF_HALO - SUBLANES
    for r in range(1, SUBLANES):
        sh_ref[r - 1] = ext_ref[r:r + span, :]

    first = CONF_HALO - (D_CONV - 1)
    for c in range(BRANCH_WIDTH // LANES):
        cols = slice(c * LANES, (c + 1) * LANES)
        taps = [jnp.broadcast_to(cw_ref[j:j + 1, cols], (SUBLANES, LANES)) for j in range(D_CONV)]
        bias = jnp.broadcast_to(cb_ref[:, cols], (SUBLANES, LANES))

        def body(it, carry):
            base = pl.multiple_of(it * CONF_ROWS_PER_ITER, CONF_ROWS_PER_ITER)
            for u in range(CONF_ROWS_PER_ITER // SUBLANES):
                parts = [bias, None]
                for j in range(D_CONV):
                    r = (first + j) % SUBLANES
                    row = base + u * SUBLANES + (first + j) - r
                    src = ext_ref if r == 0 else sh_ref.at[r - 1]
                    term = taps[j] * src[pl.ds(row, SUBLANES), cols]
                    parts[j % 2] = term if parts[j % 2] is None else parts[j % 2] + term
                y_ref[pl.ds(base + u * SUBLANES, SUBLANES), cols] = parts[0] + parts[1]
            return carry

        lax.fori_loop(0, ts // CONF_ROWS_PER_ITER, body, 0)

    ext_ref[0:CONF_HALO, :] = ext_ref[ts:ts + CONF_HALO, :]
    acc = y_ref[...]
    mu = jnp.mean(acc, axis=1, keepdims=True)
    xc = acc - mu
    var = jnp.mean(xc * xc, axis=1, keepdims=True)
    y = xc * lax.rsqrt(var + NORM_EPS) * lnw_ref[...] + lnb_ref[...]
    o_ref[...] = (_silu(y) * _silu(g_ref[...].astype(F32))).astype(o_ref.dtype)


def _conformer(p3b, conv_w, conv_b, ln_w, ln_b, ts):
    b, s, _ = p3b.shape
    ts = min(ts, s)
    const = lambda bi, t: (0, 0)
    w = BRANCH_WIDTH
    return pl.pallas_call(
        functools.partial(_conformer_kernel, ts=ts),
        grid=(b, s // ts),
        in_specs=[
            pl.BlockSpec((None, ts, w), lambda bi, t: (bi, t, COL2_DGLU // w)),
            pl.BlockSpec((None, ts, w), lambda bi, t: (bi, t, COL2_DGLU // w + 1)),
            pl.BlockSpec((None, ts, w), lambda bi, t: (bi, t, COL2_DG // w)),
            pl.BlockSpec((D_CONV, w), const),
            pl.BlockSpec((1, w), const),
            pl.BlockSpec((1, w), const),
            pl.BlockSpec((1, w), const),
        ],
        out_specs=pl.BlockSpec((None, ts, w), lambda bi, t: (bi, t, 0)),
        out_shape=jax.ShapeDtypeStruct((b, s, w), BF16),
        scratch_shapes=[pltpu.VMEM((CONF_HALO + ts, w), F32),
                        pltpu.VMEM((SUBLANES - 1, CONF_HALO + ts - SUBLANES, w), F32),
                        pltpu.VMEM((ts, w), F32)],
        compiler_params=_cparams("parallel", "arbitrary"),
        name="conformer",
    )(p3b, p3b, p3b, conv_w, conv_b.reshape(1, w), ln_w.reshape(1, w), ln_b.reshape(1, w))


def _merge_kernel(h_ref, ya_ref, yb_ref, yc_ref, yd_ref, wg0, wg1, wg2, wg3, wb_ref, o_ref):
    h = h_ref[...]
    acc = None
    for i, (y_ref, wg_ref) in enumerate(((ya_ref, wg0), (yb_ref, wg1), (yc_ref, wg2), (yd_ref, wg3))):
        gate = lax.dot_general(h, wg_ref[...], DN_NT, preferred_element_type=F32)
        up = jnp.dot(y_ref[...], wb_ref[i], preferred_element_type=F32)
        term = _sigmoid(gate) * up
        acc = term if acc is None else acc + term
    o_ref[...] = acc.astype(o_ref.dtype)


def _merge(h2, ys, wgt, row0, wb, layer, tm, tn):
    t, d = h2.shape
    tm = min(tm, t)
    nblk = d // tn
    y_spec = pl.BlockSpec((tm, BRANCH_WIDTH), lambda i, j: (i, 0))
    wg_specs = [pl.BlockSpec((pl.Element(tn), pl.Element(d)),
                             functools.partial(
                                 lambda i, j, k: (pl.multiple_of(row0 + (k * nblk + j) * tn,
                                                                 BF16_SUBLANES), 0), k=k))
                for k in range(N_BRANCH)]
    return pl.pallas_call(
        _merge_kernel,
        grid=(t // tm, nblk),
        in_specs=[pl.BlockSpec((tm, d), lambda i, j: (i, 0)), y_spec, y_spec, y_spec, y_spec,
                  *wg_specs,
                  pl.BlockSpec((None, N_BRANCH, BRANCH_WIDTH, tn), lambda i, j: (layer, 0, 0, j))],
        out_specs=pl.BlockSpec((tm, tn), lambda i, j: (i, j)),
        out_shape=jax.ShapeDtypeStruct((t, d), BF16),
        compiler_params=_cparams("parallel", "parallel"),
        name="merge",
    )(h2, *ys, wgt, wgt, wgt, wgt, wb)


def _outproj_kernel(m_ref, w_ref, x_ref, nw_ref, *out_refs, last):
    xn = x_ref[...] + jnp.dot(m_ref[...], w_ref[...], preferred_element_type=F32)
    ms = jnp.mean(xn * xn, axis=-1, keepdims=True)
    normed = xn * lax.rsqrt(ms + NORM_EPS) * nw_ref[...]
    if last:
        out_refs[0][...] = normed
    else:
        out_refs[0][...] = xn
        out_refs[1][...] = normed.astype(BF16)


def _outproj(merged, w_out, layer, x2, next_norm_w, last, tm):
    t, d = x2.shape
    tm = min(tm, t)
    row = pl.BlockSpec((tm, d), lambda i: (i, 0))
    if last:
        out_specs = row
        out_shape = jax.ShapeDtypeStruct((t, d), F32)
    else:
        out_specs = (row, row)
        out_shape = (jax.ShapeDtypeStruct((t, d), F32), jax.ShapeDtypeStruct((t, d), BF16))
    return pl.pallas_call(
        functools.partial(_outproj_kernel, last=last),
        grid=(t // tm,),
        in_specs=[row, pl.BlockSpec((None, d, d), lambda i: (layer, 0, 0)), row,
                  pl.BlockSpec((1, d), lambda i: (0, 0))],
        out_specs=out_specs,
        out_shape=out_shape,
        compiler_params=_cparams("parallel"),
        name="outproj",
    )(merged, w_out, x2, next_norm_w.reshape(1, d))


def kernel(x, norm_w, w_in, diff_lambda, diff_subln_w, swa_sinks, ssd_conv_w, ssd_conv_b,
           ssd_dt_bias, ssd_a_log, ssd_d, ssd_norm_w, conf_conv_w, conf_conv_b, conf_ln_w,
           conf_ln_b, w_branch, w_out, rel_bias, final_norm_w):
    b, s, d = x.shape
    depth = w_in.shape[0]
    t = b * s
    tq = min(256, s)
    bias_a = _bias_tiles(rel_bias, tq, 0, A_HEADS, None)
    bias_b = _bias_tiles(rel_bias, WINDOW, A_HEADS, B_HEADS, WINDOW)

    first_scale = np.ones((1, N_FIRST), np.float32)
    first_scale[0, :A_HEADS * 2 * A_QK] = A_QK ** -0.5 * LOG2E
    first_scale[0, COL_BQ:COL_BQ + B_HEADS * B_HD] = B_HD ** -0.5 * LOG2E
    first_scale = jnp.asarray(first_scale)
    ones = lambda n: jnp.ones((1, n), F32)

    n_in = w_in.shape[2]
    w_t = jnp.swapaxes(w_in, 1, 2).astype(BF16).reshape(depth * n_in, d)
    n_second = SRC_SECOND[1] - SRC_SECOND[0]
    w_br = w_branch.astype(BF16)
    w_o = w_out.astype(BF16)

    x2 = x.reshape(t, d)
    h2 = _rmsnorm(x2, norm_w[0], BF16)
    out = None
    for l in range(depth):
        first = _matmul(h2, w_t, l * n_in, first_scale, BF16, 1024, 1024, "inproj_first")
        second = _matmul(h2, w_t, l * n_in + SRC_SECOND[0], ones(n_second), BF16, 1024, 1024,
                         "inproj_second")
        dt_raw = _matmul(h2, w_t, l * n_in + SRC_DT[0], ones(LANES), F32, 1024, LANES, "dtproj")
        p3 = first.reshape(b, s, N_FIRST)
        p3b = second.reshape(b, s, n_second)
        dt3 = dt_raw.reshape(b, s, LANES)

        lam_init = 0.8 - 0.6 * math.exp(-0.3 * l)
        ya = _diff_attention(p3, diff_lambda[l], diff_subln_w[l], rel_bias, bias_a, lam_init, tq)
        yb = _swa(p3, swa_sinks[l], bias_b)
        yc = _ssd(p3, p3b, dt3, ssd_conv_w[l], ssd_conv_b[l], ssd_dt_bias[l], ssd_a_log[l],
                  ssd_d[l], ssd_norm_w[l])
        yd = _conformer(p3b, conf_conv_w[l], conf_conv_b[l], conf_ln_w[l], conf_ln_b[l], 256)

        ys = [y.reshape(t, BRANCH_WIDTH) for y in (ya, yb, yc, yd)]
        merged = _merge(h2, ys, w_t, l * n_in + SRC_MG, w_br, l, 1024, 256)
        last = l == depth - 1
        next_w = final_norm_w if last else norm_w[l + 1]
        res = _outproj(merged, w_o, l, x2, next_w, last, 512)
        if last:
            out = res
        else:
            x2, h2 = res
    return out.reshape(b, s, d)
```
